```python
import math
import jax, jax.numpy as jnp
from jax import lax
import numpy as np

D_MODEL = 4096
BATCH = 4
SEQ = 2048
DEPTH = 1

D_MIX = D_MODEL
POOL_WINDOWS = (2, 4, 8, 16)
N_POOL_GROUPS = len(POOL_WINDOWS)
D_POOL = D_MIX // 4
POOL_GROUP = D_POOL // N_POOL_GROUPS
D_ATTN = D_MIX - D_POOL
HEAD_DIM = 128
N_HEADS = D_ATTN // (2 * HEAD_DIM)
D_QK = N_HEADS * 2 * HEAD_DIM
D_V = N_HEADS * 2 * HEAD_DIM
D_IN = D_POOL + 2 * D_QK + D_V
Q_BLOCK = 128
D_FF = ((8 * D_MODEL // 3 + 255) // 256) * 256
N_SUB = 3
N_MOD = 3
NORM_EPS = 1e-6

kernel_name = "hybrid_pool_diffattn_macaron_layer"


def rms_norm(x, g, eps=NORM_EPS):
    xf = x.astype(jnp.float32)
    y = xf * lax.rsqrt(jnp.mean(xf * xf, axis=-1, keepdims=True) + eps)
    return (y * g.astype(jnp.float32)).astype(x.dtype)


def modulate(h, shift, scale):
    return h * (1.0 + scale[:, None, :]) + shift[:, None, :]


def swiglu(h, w1, w3, w2):
    return (jax.nn.silu(h @ w1) * (h @ w3)) @ w2


def pool_mixer(u, w_pool, pool_scale):
    b, s, _ = u.shape
    ug = u.reshape(b, s, N_POOL_GROUPS, POOL_GROUP).astype(jnp.float32)
    cs = jnp.pad(jnp.cumsum(ug, axis=1), ((0, 0), (1, 0), (0, 0), (0, 0)))
    win = jnp.array(POOL_WINDOWS, dtype=jnp.int32)
    t = jnp.arange(s, dtype=jnp.int32)[:, None]
    lo = jnp.maximum(t + 1 - win[None, :], 0)
    grp = jnp.arange(N_POOL_GROUPS, dtype=jnp.int32)[None, :]
    window_sum = cs[:, 1:] - cs[:, lo, grp]
    count = jnp.minimum(t + 1, win[None, :]).astype(jnp.float32)
    y = window_sum / count[None, :, :, None] - ug
    y = jnp.einsum('bsgc,gcd->bsgd', y.astype(u.dtype), w_pool)
    return y.reshape(b, s, D_POOL) * pool_scale


def diff_attention(q, k, v, lam, subln_g, lambda_init):
    b, s = q.shape[:2]
    n_blk = s // Q_BLOCK
    qb = (q * (HEAD_DIM ** -0.5)).reshape(b, n_blk, Q_BLOCK, N_HEADS, 2, HEAD_DIM)
    qb = qb.transpose(1, 0, 2, 3, 4, 5)
    key_pos = jnp.arange(s, dtype=jnp.int32)
    starts = jnp.arange(n_blk, dtype=jnp.int32) * Q_BLOCK

    def block(args):
        q_blk, start = args
        scores = jnp.einsum('bqhmd,bkhmd->bhmqk', q_blk, k).astype(jnp.float32)
        q_pos = start + jnp.arange(Q_BLOCK, dtype=jnp.int32)
        causal = key_pos[None, :] <= q_pos[:, None]
        scores = jnp.where(causal, scores, -jnp.inf)
        p = jax.nn.softmax(scores, axis=-1)
        a = p[:, :, 0] - lam * p[:, :, 1]
        return jnp.einsum('bhqk,bkhe->bqhe', a.astype(v.dtype), v)

    o = lax.map(block, (qb, starts))
    o = o.transpose(1, 0, 2, 3, 4).reshape(b, s, N_HEADS, 2 * HEAD_DIM)
    o = rms_norm(o, subln_g) * (1.0 - lambda_init)
    return o.reshape(b, s, N_HEADS * 2 * HEAD_DIM)


def setup_inputs(seed: int = 0) -> dict:
    key = jax.random.key(seed)
    ks = jax.random.split(key, 24)
    f32 = jnp.float32
    nrm = lambda k, shape, s: jax.random.normal(k, shape, f32) * s
    L = DEPTH
    return {
        "x": nrm(ks[0], (BATCH, SEQ, D_MODEL), 1.0),
        "c": nrm(ks[1], (BATCH, D_MODEL), 1.0),
        "w_ada": nrm(ks[2], (L, D_MODEL, N_SUB * N_MOD * D_MODEL), D_MODEL ** -0.5),
        "b_ada": nrm(ks[3], (L, N_SUB * N_MOD * D_MODEL), 0.01),
        "pre_norm_g": 1.0 + nrm(ks[4], (L, N_SUB, D_MODEL), 0.02),
        "post_norm_g": 1.0 + nrm(ks[5], (L, N_SUB, D_MODEL), 0.02),
        "ffn1_w1": nrm(ks[6], (L, D_MODEL, D_FF), D_MODEL ** -0.5),
        "ffn1_w3": nrm(ks[7], (L, D_MODEL, D_FF), D_MODEL ** -0.5),
        "ffn1_w2": nrm(ks[8], (L, D_FF, D_MODEL), D_FF ** -0.5),
        "w_in": nrm(ks[9], (L, D_MODEL, D_IN), D_MODEL ** -0.5),
        "w_pool": nrm(ks[10], (L, N_POOL_GROUPS, POOL_GROUP, POOL_GROUP), POOL_GROUP ** -0.5),
        "pool_scale": 1.0 + nrm(ks[11], (L, D_POOL), 0.1),
        "lambda_q1": nrm(ks[12], (L, HEAD_DIM), 0.1),
        "lambda_k1": nrm(ks[13], (L, HEAD_DIM), 0.1),
        "lambda_q2": nrm(ks[14], (L, HEAD_DIM), 0.1),
        "lambda_k2": nrm(ks[15], (L, HEAD_DIM), 0.1),
        "subln_g": 1.0 + nrm(ks[16], (L, 2 * HEAD_DIM), 0.02),
        "w_out": nrm(ks[17], (L, D_MIX, D_MODEL), D_MIX ** -0.5),
        "ffn2_w1": nrm(ks[18], (L, D_MODEL, D_FF), D_MODEL ** -0.5),
        "ffn2_w3": nrm(ks[19], (L, D_MODEL, D_FF), D_MODEL ** -0.5),
        "ffn2_w2": nrm(ks[20], (L, D_FF, D_MODEL), D_FF ** -0.5),
    }


def reference(x, c, w_ada, b_ada, pre_norm_g, post_norm_g, ffn1_w1, ffn1_w3, ffn1_w2,
              w_in, w_pool, pool_scale, lambda_q1, lambda_k1, lambda_q2, lambda_k2,
              subln_g, w_out, ffn2_w1, ffn2_w3, ffn2_w2):
    b, s, _ = x.shape
    for l in range(DEPTH):
        lambda_init = 0.8 - 0.6 * math.exp(-0.3 * l)
        mod = (jax.nn.silu(c) @ w_ada[l] + b_ada[l]).reshape(b, N_SUB, N_MOD, D_MODEL)
        shift, scale, gate = mod[:, :, 0], mod[:, :, 1], mod[:, :, 2]

        h = modulate(rms_norm(x, pre_norm_g[l, 0]), shift[:, 0], scale[:, 0])
        f = rms_norm(swiglu(h, ffn1_w1[l], ffn1_w3[l], ffn1_w2[l]), post_norm_g[l, 0])
        x = x + 0.5 * gate[:, 0, None, :] * f

        h = modulate(rms_norm(x, pre_norm_g[l, 1]), shift[:, 1], scale[:, 1])
        proj = h @ w_in[l]
        u = proj[..., :D_POOL]
        q = proj[..., D_POOL:D_POOL + D_QK].reshape(b, s, N_HEADS, 2, HEAD_DIM)
        k = proj[..., D_POOL + D_QK:D_POOL + 2 * D_QK].reshape(b, s, N_HEADS, 2, HEAD_DIM)
        v = proj[..., D_POOL + 2 * D_QK:].reshape(b, s, N_HEADS, 2 * HEAD_DIM)
        lam = (jnp.exp(jnp.sum(lambda_q1[l].astype(jnp.float32) * lambda_k1[l].astype(jnp.float32)))
               - jnp.exp(jnp.sum(lambda_q2[l].astype(jnp.float32) * lambda_k2[l].astype(jnp.float32)))
               + lambda_init)
        y_pool = pool_mixer(u, w_pool[l], pool_scale[l])
        y_attn = diff_attention(q, k, v, lam, subln_g[l], lambda_init)
        mixed = jnp.concatenate([y_pool, y_attn.astype(y_pool.dtype)], axis=-1) @ w_out[l]
        x = x + gate[:, 1, None, :] * rms_norm(mixed, post_norm_g[l, 1])

        h = modulate(rms_norm(x, pre_norm_g[l, 2]), shift[:, 2], scale[:, 2])
        f = rms_norm(swiglu(h, ffn2_w1[l], ffn2_w3[l], ffn2_w2[l]), post_norm_g[l, 2])
        x = x + 0.5 * gate[:, 2, None, :] * f
    return x
```

```python
import functools
import math

import jax
import jax.numpy as jnp
from jax import lax
from jax.experimental import pallas as pl
from jax.experimental.pallas import tpu as pltpu

F32 = jnp.float32
BF16 = jnp.bfloat16

NORM_EPS = 1e-6
POOL_WINDOWS = (2, 4, 8, 16)
N_SUB = 3
N_MOD = 3
V7X_VMEM_LIMIT_BYTES = 58 * 1024 * 1024
ROW_CHUNK = 64


def _compiler_params(n_grid_dims):
    return pltpu.CompilerParams(
        dimension_semantics=("arbitrary",) * n_grid_dims,
        vmem_limit_bytes=V7X_VMEM_LIMIT_BYTES,
    )


def _rms(xf):
    return xf * lax.rsqrt(jnp.mean(xf * xf, axis=-1, keepdims=True) + NORM_EPS)


def _for_row_chunks(n_rows, body):
    chunk = min(ROW_CHUNK, n_rows)

    def step(r, carry):
        body(pl.ds(pl.multiple_of(r * chunk, chunk), chunk))
        return carry

    lax.fori_loop(0, n_rows // chunk, step, 0)


def _ada_kernel(c_ref, w_ref, b_ref, o_ref):
    c = c_ref[...]
    act = (c * jax.nn.sigmoid(c)).astype(BF16)
    o_ref[...] = jnp.dot(act, w_ref[...].astype(BF16), preferred_element_type=F32) + b_ref[...]


def _ada(c, w_ada, b_ada):
    b, d = c.shape
    n = w_ada.shape[1]
    rows = 8
    tn = 512 if n % 512 == 0 else n
    c_pad = jnp.pad(c, ((0, rows - b), (0, 0)))
    out = pl.pallas_call(
        _ada_kernel,
        grid=(n // tn,),
        in_specs=[
            pl.BlockSpec((rows, d), lambda j: (0, 0)),
            pl.BlockSpec((d, tn), lambda j: (0, j)),
            pl.BlockSpec((1, tn), lambda j: (0, j)),
        ],
        out_specs=pl.BlockSpec((rows, tn), lambda j: (0, j)),
        out_shape=jax.ShapeDtypeStruct((rows, n), F32),
        compiler_params=_compiler_params(1),
        name="ada",
    )(c_pad, w_ada, b_ada.reshape(1, n))
    return out[:b]


def _ffn_kernel(x_ref, shift_ref, scale_ref, gate_ref, pre_g_ref, post_g_ref,
                w1_ref, w3_ref, w2_ref, o_ref, h_ref):
    j = pl.program_id(1)
    tm = x_ref.shape[0]

    @pl.when(j == 0)
    def _():
        shift = shift_ref[0]
        scale1 = 1.0 + scale_ref[0]
        g = pre_g_ref[...]

        def body(rows):
            y = _rms(x_ref[rows, :]) * g
            h_ref[rows, :] = (y * scale1 + shift).astype(BF16)

        _for_row_chunks(tm, body)
        o_ref[...] = jnp.zeros_like(o_ref)

    h = h_ref[...]
    a = jnp.dot(h, w1_ref[...], preferred_element_type=F32)
    b = jnp.dot(h, w3_ref[...], preferred_element_type=F32)
    act = (a * jax.nn.sigmoid(a) * b).astype(BF16)
    o_ref[...] += jnp.dot(act, w2_ref[...], preferred_element_type=F32)

    @pl.when(j == pl.num_programs(1) - 1)
    def _():
        half_gate = 0.5 * gate_ref[0]
        pg = post_g_ref[...]

        def body(rows):
            f = _rms(o_ref[rows, :]) * pg
            o_ref[rows, :] = x_ref[rows, :] + half_gate * f

        _for_row_chunks(tm, body)


def _ffn(x, shift, scale, gate, pre_g, post_g, w1, w3, w2, seq):
    m, d = x.shape
    dff = w1.shape[1]
    tm = min(512, seq)
    tf = 256 if dff % 256 == 0 else dff
    row_vec = pl.BlockSpec((1, d), lambda i, j: (0, 0))
    batch_vec = pl.BlockSpec((1, 1, d), lambda i, j: (i * tm // seq, 0, 0))
    return pl.pallas_call(
        _ffn_kernel,
        grid=(m // tm, dff // tf),
        in_specs=[
            pl.BlockSpec((tm, d), lambda i, j: (i, 0)),
            batch_vec, batch_vec, batch_vec,
            row_vec, row_vec,
            pl.BlockSpec((d, tf), lambda i, j: (0, j)),
            pl.BlockSpec((d, tf), lambda i, j: (0, j)),
            pl.BlockSpec((tf, d), lambda i, j: (j, 0)),
        ],
        out_specs=pl.BlockSpec((tm, d), lambda i, j: (i, 0)),
        out_shape=jax.ShapeDtypeStruct((m, d), F32),
        scratch_shapes=[pltpu.VMEM((tm, d), BF16)],
        compiler_params=_compiler_params(2),
        name="ffn",
    )(x, shift, scale, gate, pre_g, post_g, w1, w3, w2)


def _proj_kernel(x_ref, shift_ref, scale_ref, pre_g_ref, w_ref, u_ref, qkv_ref, h_ref,
                 *, n_q_tiles, q_scale):
    j = pl.program_id(1)
    tm = x_ref.shape[0]

    @pl.when(j == 0)
    def _():
        shift = shift_ref[0]
        scale1 = 1.0 + scale_ref[0]
        g = pre_g_ref[...]

        def body(rows):
            y = _rms(x_ref[rows, :]) * g
            h_ref[rows, :] = (y * scale1 + shift).astype(BF16)

        _for_row_chunks(tm, body)

    res = jnp.dot(h_ref[...], w_ref[...], preferred_element_type=F32)

    @pl.when(j == 0)
    def _():
        u_ref[...] = res

    @pl.when(j > 0)
    def _():
        factor = jnp.where(j <= n_q_tiles, q_scale, 1.0).astype(F32)
        qkv_ref[...] = (res * factor).astype(BF16)


def _proj(x, shift, scale, pre_g, w_in, d_pool, d_qk, head_dim, seq):
    m, d = x.shape
    d_in = w_in.shape[1]
    tn = d_pool
    assert d_in % tn == 0 and d_qk % tn == 0
    tm = min(512, seq)
    row_vec = pl.BlockSpec((1, d), lambda i, j: (0, 0))
    batch_vec = pl.BlockSpec((1, 1, d), lambda i, j: (i * tm // seq, 0, 0))
    kern = functools.partial(_proj_kernel, n_q_tiles=d_qk // tn, q_scale=head_dim ** -0.5)
    return pl.pallas_call(
        kern,
        grid=(m // tm, d_in // tn),
        in_specs=[
            pl.BlockSpec((tm, d), lambda i, j: (i, 0)),
            batch_vec, batch_vec, row_vec,
            pl.BlockSpec((d, tn), lambda i, j: (0, j)),
        ],
        out_specs=[
            pl.BlockSpec((tm, tn), lambda i, j: (i, 0)),
            pl.BlockSpec((tm, tn), lambda i, j: (i, jnp.maximum(j - 1, 0))),
        ],
        out_shape=[
            jax.ShapeDtypeStruct((m, d_pool), F32),
            jax.ShapeDtypeStruct((m, d_in - d_pool), BF16),
        ],
        scratch_shapes=[pltpu.VMEM((tm, d), BF16)],
        compiler_params=_compiler_params(2),
        name="proj",
    )(x, shift, scale, pre_g, w_in)


def _pool_kernel(u_ref, w_ref, ps_ref, o_ref):
    seq = u_ref.shape[0]
    n_groups = len(POOL_WINDOWS)
    cg = u_ref.shape[1] // n_groups
    t = lax.broadcasted_iota(jnp.int32, (seq, cg), 0)
    for g, win in enumerate(POOL_WINDOWS):
        cols = slice(g * cg, (g + 1) * cg)
        ug = u_ref[:, cols]
        s = ug
        k = 1
        while k < win:
            s = s + jnp.where(t >= k, pltpu.roll(s, k, axis=0), 0.0)
            k *= 2
        count = jnp.minimum(t + 1, win).astype(F32)
        y = (s / count - ug).astype(BF16)
        yo = jnp.dot(y, w_ref[g], preferred_element_type=F32) * ps_ref[:, cols]
        o_ref[:, cols] = yo.astype(BF16)


def _pool(u, w_pool, pool_scale, seq):
    m, d_pool = u.shape
    return pl.pallas_call(
        _pool_kernel,
        grid=(m // seq,),
        in_specs=[
            pl.BlockSpec((seq, d_pool), lambda b: (b, 0)),
            pl.BlockSpec(w_pool.shape, lambda b: (0, 0, 0)),
            pl.BlockSpec((1, d_pool), lambda b: (0, 0)),
        ],
        out_specs=pl.BlockSpec((seq, d_pool), lambda b: (b, 0)),
        out_shape=jax.ShapeDtypeStruct((m, d_pool), BF16),
        compiler_params=_compiler_params(1),
        name="pool",
    )(u, w_pool, pool_scale.reshape(1, d_pool))


def _attn_kernel(lq1_ref, lk1_ref, lq2_ref, lk2_ref, sg_ref, q_ref, k_ref, v_ref, o_ref,
                 *, lambda_init, tq):
    seq = q_ref.shape[0]
    hd = q_ref.shape[1] // 2
    lam = (jnp.exp(jnp.sum(lq1_ref[...] * lk1_ref[...], axis=-1, keepdims=True))
           - jnp.exp(jnp.sum(lq2_ref[...] * lk2_ref[...], axis=-1, keepdims=True))
           + lambda_init)
    sg = sg_ref[...]
    nt_dims = (((1,), (1,)), ((), ()))
    for i in range(seq // tq):
        kv = (i + 1) * tq
        rows = slice(i * tq, (i + 1) * tq)
        q_pos = lax.broadcasted_iota(jnp.int32, (tq, kv), 0) + i * tq
        k_pos = lax.broadcasted_iota(jnp.int32, (tq, kv), 1)
        causal = k_pos <= q_pos

        def softmax_parts(cols):
            s = lax.dot_general(q_ref[rows, cols], k_ref[0:kv, cols], nt_dims,
                                preferred_element_type=F32)
            s = jnp.where(causal, s, -jnp.inf)
            p = jnp.exp(s - jnp.max(s, axis=-1, keepdims=True))
            return p, jnp.sum(p, axis=-1, keepdims=True)

        p1, l1 = softmax_parts(slice(0, hd))
        p2, l2 = softmax_parts(slice(hd, 2 * hd))
        a = p1 * (1.0 / l1) - p2 * (lam / l2)
        o = jnp.dot(a.astype(BF16), v_ref[0:kv, :], preferred_element_type=F32)
        o_ref[rows, :] = ((_rms(o) * sg) * (1.0 - lambda_init)).astype(BF16)


def _attn(qkv, lq1, lk1, lq2, lk2, subln_g, n_heads, head_dim, seq, lambda_init):
    m = qkv.shape[0]
    hw = 2 * head_dim
    vec = lambda n: pl.BlockSpec((1, n), lambda b, h: (0, 0))
    kern = functools.partial(_attn_kernel, lambda_init=lambda_init, tq=min(256, seq))
    return pl.pallas_call(
        kern,
        grid=(m // seq, n_heads),
        in_specs=[
            vec(head_dim), vec(head_dim), vec(head_dim), vec(head_dim), vec(hw),
            pl.BlockSpec((seq, hw), lambda b, h: (b, h)),
            pl.BlockSpec((seq, hw), lambda b, h: (b, n_heads + h)),
            pl.BlockSpec((seq, hw), lambda b, h: (b, 2 * n_heads + h)),
        ],
        out_specs=pl.BlockSpec((seq, hw), lambda b, h: (b, h)),
        out_shape=jax.ShapeDtypeStruct((m, n_heads * hw), BF16),
        compiler_params=_compiler_params(2),
        name="attn",
    )(lq1.reshape(1, -1), lk1.reshape(1, -1), lq2.reshape(1, -1), lk2.reshape(1, -1),
      subln_g.reshape(1, -1), qkv, qkv, qkv)


def _outproj_kernel(yp_ref, ya_ref, w_ref, x_ref, gate_ref, post_g_ref, o_ref, *, n_pool_tiles):
    k = pl.program_id(1)
    tm = x_ref.shape[0]

    @pl.when(k == 0)
    def _():
        o_ref[...] = jnp.zeros_like(o_ref)

    @pl.when(k < n_pool_tiles)
    def _():
        o_ref[...] += jnp.dot(yp_ref[...], w_ref[...], preferred_element_type=F32)

    @pl.when(k >= n_pool_tiles)
    def _():
        o_ref[...] += jnp.dot(ya_ref[...], w_ref[...], preferred_element_type=F32)

    @pl.when(k == pl.num_programs(1) - 1)
    def _():
        gate = gate_ref[0]
        pg = post_g_ref[...]

        def body(rows):
            f = _rms(o_ref[rows, :]) * pg
            o_ref[rows, :] = x_ref[rows, :] + gate * f

        _for_row_chunks(tm, body)


def _outproj(y_pool, y_attn, w_out, x, gate, post_g, seq):
    m, d = x.shape
    d_pool = y_pool.shape[1]
    d_mix = w_out.shape[0]
    tk = min(1024, d_pool)
    assert d_pool % tk == 0 and d_mix % tk == 0
    n_pool_tiles = d_pool // tk
    tm = min(512, seq)
    kern = functools.partial(_outproj_kernel, n_pool_tiles=n_pool_tiles)
    return pl.pallas_call(
        kern,
        grid=(m // tm, d_mix // tk),
        in_specs=[
            pl.BlockSpec((tm, tk), lambda i, k: (i, jnp.minimum(k, n_pool_tiles - 1))),
            pl.BlockSpec((tm, tk), lambda i, k: (i, jnp.maximum(k - n_pool_tiles, 0))),
            pl.BlockSpec((tk, d), lambda i, k: (k, 0)),
            pl.BlockSpec((tm, d), lambda i, k: (i, 0)),
            pl.BlockSpec((1, 1, d), lambda i, k: (i * tm // seq, 0, 0)),
            pl.BlockSpec((1, d), lambda i, k: (0, 0)),
        ],
        out_specs=pl.BlockSpec((tm, d), lambda i, k: (i, 0)),
        out_shape=jax.ShapeDtypeStruct((m, d), F32),
        compiler_params=_compiler_params(2),
        name="outproj",
    )(y_pool, y_attn, w_out, x, gate, post_g)


def kernel(x, c, w_ada, b_ada, pre_norm_g, post_norm_g, ffn1_w1, ffn1_w3, ffn1_w2, w_in, w_pool,
           pool_scale, lambda_q1, lambda_k1, lambda_q2, lambda_k2, subln_g, w_out, ffn2_w1,
           ffn2_w3, ffn2_w2):
    b, s, d = x.shape
    depth = w_ada.shape[0]
    d_pool = pool_scale.shape[1]
    head_dim = lambda_q1.shape[1]
    d_in = w_in.shape[2]
    d_qk = (d_in - d_pool) // 3
    n_heads = d_qk // (2 * head_dim)
    xr = x.reshape(b * s, d)
    for l in range(depth):
        lambda_init = 0.8 - 0.6 * math.exp(-0.3 * l)
        mod = _ada(c, w_ada[l], b_ada[l]).reshape(b, N_SUB, N_MOD, 1, d)
        shift, scale, gate = mod[:, :, 0], mod[:, :, 1], mod[:, :, 2]
        pre_g = pre_norm_g[l].reshape(N_SUB, 1, d)
        post_g = post_norm_g[l].reshape(N_SUB, 1, d)

        xr = _ffn(xr, shift[:, 0], scale[:, 0], gate[:, 0], pre_g[0], post_g[0],
                  ffn1_w1[l].astype(BF16), ffn1_w3[l].astype(BF16), ffn1_w2[l].astype(BF16), s)

        u, qkv = _proj(xr, shift[:, 1], scale[:, 1], pre_g[1], w_in[l].astype(BF16),
                       d_pool, d_qk, head_dim, s)
        y_pool = _pool(u, w_pool[l].astype(BF16), pool_scale[l], s)
        y_attn = _attn(qkv, lambda_q1[l], lambda_k1[l], lambda_q2[l], lambda_k2[l], subln_g[l],
                       n_heads, head_dim, s, lambda_init)
        xr = _outproj(y_pool, y_attn, w_out[l].astype(BF16), xr, gate[:, 1], post_g[1], s)

        xr = _ffn(xr, shift[:, 2], scale[:, 2], gate[:, 2], pre_g[2], post_g[2],
                  ffn2_w1[l].astype(BF16), ffn2_w3[l].astype(BF16), ffn2_w2[l].astype(BF16), s)
    return xr.reshape(b, s, d)
```

```python
import functools
import math

import jax
import jax.numpy as jnp
from jax import lax
from jax.experimental import pallas as pl
from jax.experimental.pallas import tpu as pltpu

F32 = jnp.float32
BF16 = jnp.bfloat16

NORM_EPS = 1e-6
POOL_WINDOWS = (2, 4, 8, 16)
N_SUB = 3
N_MOD = 3
V7X_VMEM_LIMIT_BYTES = 58 * 1024 * 1024
NORM_ROWS = 32
EPILOGUE_ROWS = 128


def _compiler_params(n_grid_dims):
    return pltpu.CompilerParams(
        dimension_semantics=("arbitrary",) * n_grid_dims,
        vmem_limit_bytes=V7X_VMEM_LIMIT_BYTES,
    )


def _inv_rms(xf):
    return lax.rsqrt(jnp.mean(xf * xf, axis=-1, keepdims=True) + NORM_EPS)


def _bdot(a, b):
    return jnp.dot(a, b, preferred_element_type=F32)


def _ada_kernel(c_ref, w_ref, b_ref, o_ref):
    c = c_ref[...]
    act = (c * jax.nn.sigmoid(c)).astype(BF16)
    o_ref[...] = _bdot(act, w_ref[...].astype(BF16)) + b_ref[...]


def _ada(c, w_ada, b_ada):
    b, d = c.shape
    n = w_ada.shape[1]
    rows = 8
    tn = 512 if n % 512 == 0 else n
    c_pad = jnp.pad(c, ((0, rows - b), (0, 0)))
    out = pl.pallas_call(
        _ada_kernel,
        grid=(n // tn,),
        in_specs=[
            pl.BlockSpec((rows, d), lambda j: (0, 0)),
            pl.BlockSpec((d, tn), lambda j: (0, j)),
            pl.BlockSpec((1, tn), lambda j: (0, j)),
        ],
        out_specs=pl.BlockSpec((rows, tn), lambda j: (0, j)),
        out_shape=jax.ShapeDtypeStruct((rows, n), F32),
        compiler_params=_compiler_params(1),
        name="ada",
    )(c_pad, w_ada, b_ada.reshape(1, n))
    return out[:b]


def _modulated_norm(xf, gs, shift):
    return ((xf * _inv_rms(xf)) * gs + shift).astype(BF16)


def _prenorm_kernel(x_ref, shift_ref, scale_ref, g_ref, h_ref):
    gs = g_ref[...] * (1.0 + scale_ref[0])
    shift = shift_ref[0]
    for r in range(0, x_ref.shape[0], NORM_ROWS):
        rows = slice(r, r + NORM_ROWS)
        h_ref[rows, :] = _modulated_norm(x_ref[rows, :], gs, shift)


def _prenorm(x, shift, scale, g, seq):
    m, d = x.shape
    tm = min(256, seq)
    return pl.pallas_call(
        _prenorm_kernel,
        grid=(m // tm,),
        in_specs=[
            pl.BlockSpec((tm, d), lambda i: (i, 0)),
            pl.BlockSpec((1, 1, d), lambda i: (i * tm // seq, 0, 0)),
            pl.BlockSpec((1, 1, d), lambda i: (i * tm // seq, 0, 0)),
            pl.BlockSpec((1, d), lambda i: (0, 0)),
        ],
        out_specs=pl.BlockSpec((tm, d), lambda i: (i, 0)),
        out_shape=jax.ShapeDtypeStruct((m, d), BF16),
        compiler_params=_compiler_params(1),
        name="prenorm",
    )(x, shift, scale, g)


def _ffn_up_kernel(h_ref, w1_ref, w3_ref, act_ref):
    h = h_ref[...]
    a = _bdot(h, w1_ref[...].astype(BF16))
    b = _bdot(h, w3_ref[...].astype(BF16))
    act_ref[...] = (a * jax.nn.sigmoid(a) * b).astype(BF16)


def _ffn_up(h, w1, w3, seq):
    m, d = h.shape
    dff = w1.shape[1]
    tm = min(1024, seq)
    tn = 256 if dff % 256 == 0 else dff
    return pl.pallas_call(
        _ffn_up_kernel,
        grid=(m // tm, dff // tn),
        in_specs=[
            pl.BlockSpec((tm, d), lambda i, j: (i, 0)),
            pl.BlockSpec((d, tn), lambda i, j: (0, j)),
            pl.BlockSpec((d, tn), lambda i, j: (0, j)),
        ],
        out_specs=pl.BlockSpec((tm, tn), lambda i, j: (i, j)),
        out_shape=jax.ShapeDtypeStruct((m, dff), BF16),
        compiler_params=_compiler_params(2),
        name="ffn_up",
    )(h, w1, w3)


def _down_kernel(act_ref, w_ref, x_ref, gate_ref, post_g_ref, *rest, n_k, coef, emit_h):
    if emit_h:
        shift_ref, scale_ref, g_ref, o_ref, h_ref, acc_ref = rest
    else:
        o_ref, acc_ref = rest
    k = pl.program_id(1)

    @pl.when(k == 0)
    def _():
        acc_ref[...] = jnp.zeros_like(acc_ref)

    @pl.when(k < n_k)
    def _():
        acc_ref[...] += _bdot(act_ref[...], w_ref[...].astype(BF16))

    @pl.when(k >= n_k)
    def _():
        base = (k - n_k) * EPILOGUE_ROWS
        cg = coef * gate_ref[0]
        pg = post_g_ref[...]
        if emit_h:
            gs = g_ref[...] * (1.0 + scale_ref[0])
            shift = shift_ref[0]
        for r in range(0, o_ref.shape[0], NORM_ROWS):
            rows = slice(r, r + NORM_ROWS)
            f = acc_ref[pl.ds(pl.multiple_of(base + r, NORM_ROWS), NORM_ROWS), :]
            out = x_ref[rows, :] + cg * ((f * _inv_rms(f)) * pg)
            o_ref[rows, :] = out
            if emit_h:
                h_ref[rows, :] = _modulated_norm(out, gs, shift)


def _down(act, w, x, gate, post_g, next_mod, *, coef, tk, seq):
    m, d = x.shape
    kdim = act.shape[1]
    tm = min(1024, seq)
    te = min(EPILOGUE_ROWS, tm)
    assert kdim % tk == 0 and tm % te == 0
    n_k = kdim // tk
    n_e = tm // te
    emit_h = next_mod is not None
    chunk_map = lambda i, k: (i * n_e + jnp.clip(k - n_k, 0, n_e - 1), 0)
    batch_vec = pl.BlockSpec((1, 1, d), lambda i, k: (i * tm // seq, 0, 0))
    row_vec = pl.BlockSpec((1, d), lambda i, k: (0, 0))
    in_specs = [
        pl.BlockSpec((tm, tk), lambda i, k: (i, jnp.minimum(k, n_k - 1))),
        pl.BlockSpec((tk, d), lambda i, k: (jnp.minimum(k, n_k - 1), 0)),
        pl.BlockSpec((te, d), chunk_map),
        batch_vec, row_vec,
    ]
    args = [act, w, x, gate, post_g]
    out_specs = [pl.BlockSpec((te, d), chunk_map)]
    out_shape = [jax.ShapeDtypeStruct((m, d), F32)]
    if emit_h:
        in_specs += [batch_vec, batch_vec, row_vec]
        args += list(next_mod)
        out_specs.append(pl.BlockSpec((te, d), chunk_map))
        out_shape.append(jax.ShapeDtypeStruct((m, d), BF16))
    kern = functools.partial(_down_kernel, n_k=n_k, coef=coef, emit_h=emit_h)
    res = pl.pallas_call(
        kern,
        grid=(m // tm, n_k + n_e),
        in_specs=in_specs,
        out_specs=out_specs,
        out_shape=out_shape,
        scratch_shapes=[pltpu.VMEM((tm, d), F32)],
        compiler_params=_compiler_params(2),
        name="down",
    )(*args)
    return (res[0], res[1]) if emit_h else (res[0], None)


def _proj_kernel(h_ref, w_ref, u_ref, qkv_ref, *, n_u_tiles, n_q_tiles, q_scale):
    j = pl.program_id(1)
    res = _bdot(h_ref[...], w_ref[...].astype(BF16))

    @pl.when(j < n_u_tiles)
    def _():
        u_ref[...] = res

    @pl.when(j >= n_u_tiles)
    def _():
        factor = jnp.where(j < n_u_tiles + n_q_tiles, q_scale, 1.0).astype(F32)
        qkv_ref[...] = (res * factor).astype(BF16)


def _proj(h, w_in, d_pool, d_qk, head_dim, seq):
    m, d = h.shape
    d_in = w_in.shape[1]
    tn = min(512, d_pool)
    assert d_pool % tn == 0 and d_qk % tn == 0
    n_u = d_pool // tn
    tm = min(1024, seq)
    kern = functools.partial(_proj_kernel, n_u_tiles=n_u, n_q_tiles=d_qk // tn,
                             q_scale=head_dim ** -0.5)
    return pl.pallas_call(
        kern,
        grid=(m // tm, d_in // tn),
        in_specs=[
            pl.BlockSpec((tm, d), lambda i, j: (i, 0)),
            pl.BlockSpec((d, tn), lambda i, j: (0, j)),
        ],
        out_specs=[
            pl.BlockSpec((tm, tn), lambda i, j: (i, jnp.minimum(j, n_u - 1))),
            pl.BlockSpec((tm, tn), lambda i, j: (i, jnp.maximum(j - n_u, 0))),
        ],
        out_shape=[
            jax.ShapeDtypeStruct((m, d_pool), F32),
            jax.ShapeDtypeStruct((m, d_in - d_pool), BF16),
        ],
        compiler_params=_compiler_params(2),
        name="proj",
    )(h, w_in)


def _pool_group(u_ref, w_ref, ps_ref, o_ref, group):
    win = POOL_WINDOWS[group]
    ug = u_ref[...]
    t = lax.broadcasted_iota(jnp.int32, ug.shape, 0)
    s = ug
    k = 1
    while k < win:
        s = s + jnp.where(t >= k, pltpu.roll(s, k, axis=0), 0.0)
        k *= 2
    count = jnp.minimum(t + 1, win).astype(F32)
    y = (s / count - ug).astype(BF16)
    o_ref[...] = (_bdot(y, w_ref[group].astype(BF16)) * ps_ref[...]).astype(BF16)


def _diff_attention(lam, sg_ref, q_ref, k_ref, v_ref, o_ref, lambda_init, tq):
    seq = q_ref.shape[0]
    hd = q_ref.shape[1] // 2
    sg = sg_ref[...]
    nt_dims = (((1,), (1,)), ((), ()))
    on_or_below_diag = (lax.broadcasted_iota(jnp.int32, (tq, tq), 1)
                        <= lax.broadcasted_iota(jnp.int32, (tq, tq), 0))
    for i in range(seq // tq):
        kv = (i + 1) * tq
        rows = slice(i * tq, kv)

        def exp_scores(cols):
            s = lax.dot_general(q_ref[rows, cols], k_ref[0:kv, cols], nt_dims,
                                preferred_element_type=F32)
            diag = jnp.where(on_or_below_diag, s[:, kv - tq:], -jnp.inf)
            s = diag if i == 0 else jnp.concatenate([s[:, :kv - tq], diag], axis=1)
            p = jnp.exp(s - jnp.max(s, axis=-1, keepdims=True))
            return p, jnp.sum(p, axis=-1, keepdims=True)

        p1, l1 = exp_scores(slice(0, hd))
        p2, l2 = exp_scores(slice(hd, 2 * hd))
        a = p1 - p2 * (lam * l1 / l2)
        o = _bdot(a.astype(BF16), v_ref[0:kv, :]) * (1.0 / l1)
        o_ref[rows, :] = (((o * _inv_rms(o)) * sg) * (1.0 - lambda_init)).astype(BF16)


def _mix_kernel(lq1_ref, lk1_ref, lq2_ref, lk2_ref, sg_ref, u_ref, wp_ref, ps_ref,
                q_ref, k_ref, v_ref, o_ref, *, lambda_init, tq):
    n_groups = len(POOL_WINDOWS)
    hh = pl.program_id(1)

    for g in range(n_groups):
        @pl.when(hh == g)
        def _(g=g):
            _pool_group(u_ref, wp_ref, ps_ref, o_ref, g)

    @pl.when(hh >= n_groups)
    def _():
        lam = (jnp.exp(jnp.sum(lq1_ref[...] * lk1_ref[...], axis=-1, keepdims=True))
               - jnp.exp(jnp.sum(lq2_ref[...] * lk2_ref[...], axis=-1, keepdims=True))
               + lambda_init)
        _diff_attention(lam, sg_ref, q_ref, k_ref, v_ref, o_ref, lambda_init, tq)


def _mix(u, qkv, w_pool, pool_scale, lq1, lk1, lq2, lk2, subln_g, n_heads, head_dim, seq,
         lambda_init):
    m, d_pool = u.shape
    n_groups = len(POOL_WINDOWS)
    hw = 2 * head_dim
    assert d_pool == n_groups * hw and w_pool.shape == (n_groups, hw, hw)
    vec = lambda n: pl.BlockSpec((1, n), lambda b, hh: (0, 0))
    head = lambda hh: jnp.maximum(hh - n_groups, 0)
    group = lambda hh: jnp.minimum(hh, n_groups - 1)
    kern = functools.partial(_mix_kernel, lambda_init=lambda_init, tq=min(256, seq))
    return pl.pallas_call(
        kern,
        grid=(m // seq, n_groups + n_heads),
        in_specs=[
            vec(head_dim), vec(head_dim), vec(head_dim), vec(head_dim), vec(hw),
            pl.BlockSpec((seq, hw), lambda b, hh: (b, group(hh))),
            pl.BlockSpec(w_pool.shape, lambda b, hh: (0, 0, 0)),
            pl.BlockSpec((1, hw), lambda b, hh: (0, group(hh))),
            pl.BlockSpec((seq, hw), lambda b, hh: (b, head(hh))),
            pl.BlockSpec((seq, hw), lambda b, hh: (b, n_heads + head(hh))),
            pl.BlockSpec((seq, hw), lambda b, hh: (b, 2 * n_heads + head(hh))),
        ],
        out_specs=pl.BlockSpec((seq, hw), lambda b, hh: (b, hh)),
        out_shape=jax.ShapeDtypeStruct((m, d_pool + n_heads * hw), BF16),
        compiler_params=_compiler_params(2),
        name="mix",
    )(lq1.reshape(1, -1), lk1.reshape(1, -1), lq2.reshape(1, -1), lk2.reshape(1, -1),
      subln_g.reshape(1, -1), u, w_pool, pool_scale.reshape(1, d_pool), qkv, qkv, qkv)


def kernel(x, c, w_ada, b_ada, pre_norm_g, post_norm_g, ffn1_w1, ffn1_w3, ffn1_w2, w_in, w_pool,
           pool_scale, lambda_q1, lambda_k1, lambda_q2, lambda_k2, subln_g, w_out, ffn2_w1,
           ffn2_w3, ffn2_w2):
    b, s, d = x.shape
    depth = w_ada.shape[0]
    d_pool = pool_scale.shape[1]
    head_dim = lambda_q1.shape[1]
    d_in = w_in.shape[2]
    d_qk = (d_in - d_pool) // 3
    n_heads = d_qk // (2 * head_dim)
    dff = ffn1_w1.shape[2]
    ffn_tk = 256 if dff % 256 == 0 else dff
    out_tk = min(512, d)
    xr = x.reshape(b * s, d)
    h = None
    for l in range(depth):
        lambda_init = 0.8 - 0.6 * math.exp(-0.3 * l)
        mod = _ada(c, w_ada[l], b_ada[l]).reshape(b, N_SUB, N_MOD, 1, d)
        shift, scale, gate = mod[:, :, 0], mod[:, :, 1], mod[:, :, 2]
        pre_g = pre_norm_g[l].reshape(N_SUB, 1, d)
        post_g = post_norm_g[l].reshape(N_SUB, 1, d)
        sub_mod = lambda i: (shift[:, i], scale[:, i], pre_g[i])

        if h is None:
            h = _prenorm(xr, *sub_mod(0), s)
        act = _ffn_up(h, ffn1_w1[l], ffn1_w3[l], s)
        xr, h = _down(act, ffn1_w2[l], xr, gate[:, 0], post_g[0], sub_mod(1),
                      coef=0.5, tk=ffn_tk, seq=s)

        u, qkv = _proj(h, w_in[l], d_pool, d_qk, head_dim, s)
        mixed = _mix(u, qkv, w_pool[l], pool_scale[l], lambda_q1[l], lambda_k1[l], lambda_q2[l],
                     lambda_k2[l], subln_g[l], n_heads, head_dim, s, lambda_init)
        xr, h = _down(mixed, w_out[l], xr, gate[:, 1], post_g[1], sub_mod(2),
                      coef=1.0, tk=out_tk, seq=s)

        act = _ffn_up(h, ffn2_w1[l], ffn2_w3[l], s)
        xr, h = _down(act, ffn2_w2[l], xr, gate[:, 2], post_g[2], None,
                      coef=0.5, tk=ffn_tk, seq=s)
    return xr.reshape(b, s, d)
```

```python
import functools
import math

import jax
import jax.numpy as jnp
from jax import lax
from jax.experimental import pallas as pl
from jax.experimental.pallas import tpu as pltpu

F32 = jnp.float32
BF16 = jnp.bfloat16

NORM_EPS = 1e-6
POOL_WINDOWS = (2, 4, 8, 16)
N_SUB = 3
N_MOD = 3
V7X_VMEM_LIMIT_BYTES = 58 * 1024 * 1024
NORM_ROWS = 32
ROW_TILE = 2048
EPILOGUE_ROWS = 128


def _compiler_params(n_grid_dims):
    return pltpu.CompilerParams(
        dimension_semantics=("arbitrary",) * n_grid_dims,
        vmem_limit_bytes=V7X_VMEM_LIMIT_BYTES,
    )


def _inv_rms(xf):
    return lax.rsqrt(jnp.mean(xf * xf, axis=-1, keepdims=True) + NORM_EPS)


def _bdot(a, b):
    return jnp.dot(a, b, preferred_element_type=F32)


def _ada_kernel(c_ref, w_ref, b_ref, o_ref):
    c = c_ref[...]
    act = (c * jax.nn.sigmoid(c)).astype(BF16)
    o_ref[...] = _bdot(act, w_ref[...].astype(BF16)) + b_ref[...]


def _ada(c, w_ada, b_ada):
    b, d = c.shape
    n = w_ada.shape[1]
    rows = 8
    tn = 512 if n % 512 == 0 else n
    c_pad = jnp.pad(c, ((0, rows - b), (0, 0)))
    out = pl.pallas_call(
        _ada_kernel,
        grid=(n // tn,),
        in_specs=[
            pl.BlockSpec((rows, d), lambda j: (0, 0)),
            pl.BlockSpec((d, tn), lambda j: (0, j)),
            pl.BlockSpec((1, tn), lambda j: (0, j)),
        ],
        out_specs=pl.BlockSpec((rows, tn), lambda j: (0, j)),
        out_shape=jax.ShapeDtypeStruct((rows, n), F32),
        compiler_params=_compiler_params(1),
        name="ada",
    )(c_pad, w_ada, b_ada.reshape(1, n))
    return out[:b]


def _modulated_norm(xf, gs, shift):
    return ((xf * _inv_rms(xf)) * gs + shift).astype(BF16)


def _prenorm_kernel(x_ref, shift_ref, scale_ref, g_ref, h_ref):
    gs = g_ref[...] * (1.0 + scale_ref[0])
    shift = shift_ref[0]
    for r in range(0, x_ref.shape[0], NORM_ROWS):
        rows = slice(r, r + NORM_ROWS)
        h_ref[rows, :] = _modulated_norm(x_ref[rows, :], gs, shift)


def _prenorm(x, shift, scale, g, seq):
    m, d = x.shape
    tm = min(256, seq)
    return pl.pallas_call(
        _prenorm_kernel,
        grid=(m // tm,),
        in_specs=[
            pl.BlockSpec((tm, d), lambda i: (i, 0)),
            pl.BlockSpec((1, 1, d), lambda i: (i * tm // seq, 0, 0)),
            pl.BlockSpec((1, 1, d), lambda i: (i * tm // seq, 0, 0)),
            pl.BlockSpec((1, d), lambda i: (0, 0)),
        ],
        out_specs=pl.BlockSpec((tm, d), lambda i: (i, 0)),
        out_shape=jax.ShapeDtypeStruct((m, d), BF16),
        compiler_params=_compiler_params(1),
        name="prenorm",
    )(x, shift, scale, g)


def _ffn_up_kernel(h_ref, w1_ref, w3_ref, act_ref):
    h = h_ref[...]
    a = _bdot(h, w1_ref[...].astype(BF16))
    b = _bdot(h, w3_ref[...].astype(BF16))
    act_ref[...] = (a * jax.nn.sigmoid(a) * b).astype(BF16)


def _ffn_up(h, w1, w3, seq):
    m, d = h.shape
    dff = w1.shape[1]
    tm = min(ROW_TILE, seq)
    tn = 256 if dff % 256 == 0 else dff
    return pl.pallas_call(
        _ffn_up_kernel,
        grid=(m // tm, dff // tn),
        in_specs=[
            pl.BlockSpec((tm, d), lambda i, j: (i, 0), pipeline_mode=pl.Buffered(1)),
            pl.BlockSpec((d, tn), lambda i, j: (0, j)),
            pl.BlockSpec((d, tn), lambda i, j: (0, j)),
        ],
        out_specs=pl.BlockSpec((tm, tn), lambda i, j: (i, j)),
        out_shape=jax.ShapeDtypeStruct((m, dff), BF16),
        compiler_params=_compiler_params(2),
        name="ffn_up",
    )(h, w1, w3)


def _down_kernel(act_ref, w_ref, x_ref, gate_ref, post_g_ref, *rest, n_k, coef, emit_h):
    if emit_h:
        shift_ref, scale_ref, g_ref, o_ref, h_ref, acc_ref = rest
    else:
        o_ref, acc_ref = rest
    k = pl.program_id(1)

    @pl.when(k == 0)
    def _():
        acc_ref[...] = jnp.zeros_like(acc_ref)

    @pl.when(k < n_k)
    def _():
        acc_ref[...] += _bdot(act_ref[...], w_ref[...].astype(BF16))

    @pl.when(k >= n_k)
    def _():
        base = (k - n_k) * EPILOGUE_ROWS
        cg = coef * gate_ref[0]
        pg = post_g_ref[...]
        if emit_h:
            gs = g_ref[...] * (1.0 + scale_ref[0])
            shift = shift_ref[0]
        for r in range(0, o_ref.shape[0], NORM_ROWS):
            rows = slice(r, r + NORM_ROWS)
            f = acc_ref[pl.ds(pl.multiple_of(base + r, NORM_ROWS), NORM_ROWS), :]
            out = x_ref[rows, :] + cg * ((f * _inv_rms(f)) * pg)
            o_ref[rows, :] = out
            if emit_h:
                h_ref[rows, :] = _modulated_norm(out, gs, shift)


def _down(act, w, x, gate, post_g, next_mod, *, coef, tk, seq):
    m, d = x.shape
    kdim = act.shape[1]
    tm = min(ROW_TILE, seq)
    te = min(EPILOGUE_ROWS, tm)
    assert kdim % tk == 0 and tm % te == 0
    n_k = kdim // tk
    n_e = tm // te
    emit_h = next_mod is not None
    chunk_map = lambda i, k: (i * n_e + jnp.clip(k - n_k, 0, n_e - 1), 0)
    batch_vec = pl.BlockSpec((1, 1, d), lambda i, k: (i * tm // seq, 0, 0))
    row_vec = pl.BlockSpec((1, d), lambda i, k: (0, 0))
    in_specs = [
        pl.BlockSpec((tm, tk), lambda i, k: (i, jnp.minimum(k, n_k - 1))),
        pl.BlockSpec((tk, d), lambda i, k: (jnp.minimum(k, n_k - 1), 0)),
        pl.BlockSpec((te, d), chunk_map),
        batch_vec, row_vec,
    ]
    args = [act, w, x, gate, post_g]
    out_specs = [pl.BlockSpec((te, d), chunk_map)]
    out_shape = [jax.ShapeDtypeStruct((m, d), F32)]
    if emit_h:
        in_specs += [batch_vec, batch_vec, row_vec]
        args += list(next_mod)
        out_specs.append(pl.BlockSpec((te, d), chunk_map))
        out_shape.append(jax.ShapeDtypeStruct((m, d), BF16))
    kern = functools.partial(_down_kernel, n_k=n_k, coef=coef, emit_h=emit_h)
    res = pl.pallas_call(
        kern,
        grid=(m // tm, n_k + n_e),
        in_specs=in_specs,
        out_specs=out_specs,
        out_shape=out_shape,
        scratch_shapes=[pltpu.VMEM((tm, d), F32)],
        compiler_params=_compiler_params(2),
        name="down",
    )(*args)
    return (res[0], res[1]) if emit_h else (res[0], None)


def _proj_kernel(h_ref, w_ref, u_ref, qkv_ref, *, n_u_tiles, n_q_tiles, q_scale):
    j = pl.program_id(1)
    res = _bdot(h_ref[...], w_ref[...].astype(BF16))

    @pl.when(j < n_u_tiles)
    def _():
        u_ref[...] = res

    @pl.when(j >= n_u_tiles)
    def _():
        factor = jnp.where(j < n_u_tiles + n_q_tiles, q_scale, 1.0).astype(F32)
        qkv_ref[...] = (res * factor).astype(BF16)


def _proj(h, w_in, d_pool, d_qk, head_dim, seq):
    m, d = h.shape
    d_in = w_in.shape[1]
    tn = min(512, d_pool)
    assert d_pool % tn == 0 and d_qk % tn == 0
    n_u = d_pool // tn
    tm = min(ROW_TILE, seq)
    kern = functools.partial(_proj_kernel, n_u_tiles=n_u, n_q_tiles=d_qk // tn,
                             q_scale=head_dim ** -0.5)
    return pl.pallas_call(
        kern,
        grid=(m // tm, d_in // tn),
        in_specs=[
            pl.BlockSpec((tm, d), lambda i, j: (i, 0), pipeline_mode=pl.Buffered(1)),
            pl.BlockSpec((d, tn), lambda i, j: (0, j)),
        ],
        out_specs=[
            pl.BlockSpec((tm, tn), lambda i, j: (i, jnp.minimum(j, n_u - 1))),
            pl.BlockSpec((tm, tn), lambda i, j: (i, jnp.maximum(j - n_u, 0))),
        ],
        out_shape=[
            jax.ShapeDtypeStruct((m, d_pool), F32),
            jax.ShapeDtypeStruct((m, d_in - d_pool), BF16),
        ],
        compiler_params=_compiler_params(2),
        name="proj",
    )(h, w_in)


def _pool_group(u_ref, w_ref, ps_ref, o_ref, group):
    win = POOL_WINDOWS[group]
    ug = u_ref[...]
    t = lax.broadcasted_iota(jnp.int32, ug.shape, 0)
    s = ug
    k = 1
    while k < win:
        s = s + jnp.where(t >= k, pltpu.roll(s, k, axis=0), 0.0)
        k *= 2
    count = jnp.minimum(t + 1, win).astype(F32)
    y = (s / count - ug).astype(BF16)
    o_ref[...] = (_bdot(y, w_ref[group].astype(BF16)) * ps_ref[...]).astype(BF16)


def _diff_attention(lam, sg_ref, q_ref, k_ref, v_ref, o_ref, lambda_init, tq):
    seq = q_ref.shape[0]
    hd = q_ref.shape[1] // 2
    sg = sg_ref[...]
    nt_dims = (((1,), (1,)), ((), ()))
    on_or_below_diag = (lax.broadcasted_iota(jnp.int32, (tq, tq), 1)
                        <= lax.broadcasted_iota(jnp.int32, (tq, tq), 0))
    for i in range(seq // tq):
        kv = (i + 1) * tq
        rows = slice(i * tq, kv)

        def exp_scores(cols):
            s = lax.dot_general(q_ref[rows, cols], k_ref[0:kv, cols], nt_dims,
                                preferred_element_type=F32)
            diag = jnp.where(on_or_below_diag, s[:, kv - tq:], -jnp.inf)
            s = diag if i == 0 else jnp.concatenate([s[:, :kv - tq], diag], axis=1)
            p = jnp.exp(s - jnp.max(s, axis=-1, keepdims=True))
            return p, jnp.sum(p, axis=-1, keepdims=True)

        p1, l1 = exp_scores(slice(0, hd))
        p2, l2 = exp_scores(slice(hd, 2 * hd))
        a = p1 - p2 * (lam * l1 / l2)
        o = _bdot(a.astype(BF16), v_ref[0:kv, :]) * (1.0 / l1)
        o_ref[rows, :] = (((o * _inv_rms(o)) * sg) * (1.0 - lambda_init)).astype(BF16)


def _mix_kernel(lq1_ref, lk1_ref, lq2_ref, lk2_ref, sg_ref, u_ref, wp_ref, ps_ref,
                q_ref, k_ref, v_ref, o_ref, *, lambda_init, tq):
    n_groups = len(POOL_WINDOWS)
    hh = pl.program_id(1)

    for g in range(n_groups):
        @pl.when(hh == g)
        def _(g=g):
            _pool_group(u_ref, wp_ref, ps_ref, o_ref, g)

    @pl.when(hh >= n_groups)
    def _():
        lam = (jnp.exp(jnp.sum(lq1_ref[...] * lk1_ref[...], axis=-1, keepdims=True))
               - jnp.exp(jnp.sum(lq2_ref[...] * lk2_ref[...], axis=-1, keepdims=True))
               + lambda_init)
        _diff_attention(lam, sg_ref, q_ref, k_ref, v_ref, o_ref, lambda_init, tq)


def _mix(u, qkv, w_pool, pool_scale, lq1, lk1, lq2, lk2, subln_g, n_heads, head_dim, seq,
         lambda_init):
    m, d_pool = u.shape
    n_groups = len(POOL_WINDOWS)
    hw = 2 * head_dim
    assert d_pool == n_groups * hw and w_pool.shape == (n_groups, hw, hw)
    vec = lambda n: pl.BlockSpec((1, n), lambda b, hh: (0, 0))
    head = lambda hh: jnp.maximum(hh - n_groups, 0)
    group = lambda hh: jnp.minimum(hh, n_groups - 1)
    kern = functools.partial(_mix_kernel, lambda_init=lambda_init, tq=min(256, seq))
    return pl.pallas_call(
        kern,
        grid=(m // seq, n_groups + n_heads),
        in_specs=[
            vec(head_dim), vec(head_dim), vec(head_dim), vec(head_dim), vec(hw),
            pl.BlockSpec((seq, hw), lambda b, hh: (b, group(hh))),
            pl.BlockSpec(w_pool.shape, lambda b, hh: (0, 0, 0)),
            pl.BlockSpec((1, hw), lambda b, hh: (0, group(hh))),
            pl.BlockSpec((seq, hw), lambda b, hh: (b, head(hh))),
            pl.BlockSpec((seq, hw), lambda b, hh: (b, n_heads + head(hh))),
            pl.BlockSpec((seq, hw), lambda b, hh: (b, 2 * n_heads + head(hh))),
        ],
        out_specs=pl.BlockSpec((seq, hw), lambda b, hh: (b, hh)),
        out_shape=jax.ShapeDtypeStruct((m, d_pool + n_heads * hw), BF16),
        compiler_params=_compiler_params(2),
        name="mix",
    )(lq1.reshape(1, -1), lk1.reshape(1, -1), lq2.reshape(1, -1), lk2.reshape(1, -1),
      subln_g.reshape(1, -1), u, w_pool, pool_scale.reshape(1, d_pool), qkv, qkv, qkv)


def kernel(x, c, w_ada, b_ada, pre_norm_g, post_norm_g, ffn1_w1, ffn1_w3, ffn1_w2, w_in, w_pool,
           pool_scale, lambda_q1, lambda_k1, lambda_q2, lambda_k2, subln_g, w_out, ffn2_w1,
           ffn2_w3, ffn2_w2):
    b, s, d = x.shape
    depth = w_ada.shape[0]
    d_pool = pool_scale.shape[1]
    head_dim = lambda_q1.shape[1]
    d_in = w_in.shape[2]
    d_qk = (d_in - d_pool) // 3
    n_heads = d_qk // (2 * head_dim)
    dff = ffn1_w1.shape[2]
    ffn_tk = 256 if dff % 256 == 0 else dff
    out_tk = min(256, d)
    xr = x.reshape(b * s, d)
    h = None
    for l in range(depth):
        lambda_init = 0.8 - 0.6 * math.exp(-0.3 * l)
        mod = _ada(c, w_ada[l], b_ada[l]).reshape(b, N_SUB, N_MOD, 1, d)
        shift, scale, gate = mod[:, :, 0], mod[:, :, 1], mod[:, :, 2]
        pre_g = pre_norm_g[l].reshape(N_SUB, 1, d)
        post_g = post_norm_g[l].reshape(N_SUB, 1, d)
        sub_mod = lambda i: (shift[:, i], scale[:, i], pre_g[i])

        if h is None:
            h = _prenorm(xr, *sub_mod(0), s)
        act = _ffn_up(h, ffn1_w1[l], ffn1_w3[l], s)
        xr, h = _down(act, ffn1_w2[l], xr, gate[:, 0], post_g[0], sub_mod(1),
                      coef=0.5, tk=ffn_tk, seq=s)

        u, qkv = _proj(h, w_in[l], d_pool, d_qk, head_dim, s)
        mixed = _mix(u, qkv, w_pool[l], pool_scale[l], lambda_q1[l], lambda_k1[l], lambda_q2[l],
                     lambda_k2[l], subln_g[l], n_heads, head_dim, s, lambda_init)
        xr, h = _down(mixed, w_out[l], xr, gate[:, 1], post_g[1], sub_mod(2),
                      coef=1.0, tk=out_tk, seq=s)

        act = _ffn_up(h, ffn2_w1[l], ffn2_w3[l], s)
        xr, h = _down(act, ffn2_w2[l], xr, gate[:, 2], post_g[2], None,
                      coef=0.5, tk=ffn_tk, seq=s)
    return xr.reshape(b, s, d)
```

```python
import functools
import math

import jax
import jax.numpy as jnp
from jax import lax
from jax.experimental import pallas as pl
from jax.experimental.pallas import tpu as pltpu

F32 = jnp.float32
BF16 = jnp.bfloat16

NORM_EPS = 1e-6
POOL_WINDOWS = (2, 4, 8, 16)
N_SUB = 3
N_MOD = 3
V7X_VMEM_LIMIT_BYTES = 58 * 1024 * 1024
NORM_ROWS = 32
ROW_TILE = 2048
EPILOGUE_ROWS = 128
UP_ROW_CHUNK = 512


def _compiler_params(n_grid_dims):
    return pltpu.CompilerParams(
        dimension_semantics=("arbitrary",) * n_grid_dims,
        vmem_limit_bytes=V7X_VMEM_LIMIT_BYTES,
    )


def _inv_rms(xf):
    return lax.rsqrt(jnp.mean(xf * xf, axis=-1, keepdims=True) + NORM_EPS)


def _bdot(a, b):
    return jnp.dot(a, b, preferred_element_type=F32)


def _ada_kernel(c_ref, w_ref, b_ref, o_ref):
    c = c_ref[...]
    act = (c * jax.nn.sigmoid(c)).astype(BF16)
    o_ref[...] = _bdot(act, w_ref[...].astype(BF16)) + b_ref[...]


def _ada(c, w_ada, b_ada):
    b, d = c.shape
    n = w_ada.shape[1]
    rows = 8
    tn = 512 if n % 512 == 0 else n
    c_pad = jnp.pad(c, ((0, rows - b), (0, 0)))
    out = pl.pallas_call(
        _ada_kernel,
        grid=(n // tn,),
        in_specs=[
            pl.BlockSpec((rows, d), lambda j: (0, 0)),
            pl.BlockSpec((d, tn), lambda j: (0, j)),
            pl.BlockSpec((1, tn), lambda j: (0, j)),
        ],
        out_specs=pl.BlockSpec((rows, tn), lambda j: (0, j)),
        out_shape=jax.ShapeDtypeStruct((rows, n), F32),
        compiler_params=_compiler_params(1),
        name="ada",
    )(c_pad, w_ada, b_ada.reshape(1, n))
    return out[:b]


def _modulated_norm(xf, gs, shift):
    return ((xf * _inv_rms(xf)) * gs + shift).astype(BF16)


def _prenorm_kernel(x_ref, shift_ref, scale_ref, g_ref, h_ref):
    gs = g_ref[...] * (1.0 + scale_ref[0])
    shift = shift_ref[0]
    for r in range(0, x_ref.shape[0], NORM_ROWS):
        rows = slice(r, r + NORM_ROWS)
        h_ref[rows, :] = _modulated_norm(x_ref[rows, :], gs, shift)


def _prenorm(x, shift, scale, g, seq):
    m, d = x.shape
    tm = min(256, seq)
    return pl.pallas_call(
        _prenorm_kernel,
        grid=(m // tm,),
        in_specs=[
            pl.BlockSpec((tm, d), lambda i: (i, 0)),
            pl.BlockSpec((1, 1, d), lambda i: (i * tm // seq, 0, 0)),
            pl.BlockSpec((1, 1, d), lambda i: (i * tm // seq, 0, 0)),
            pl.BlockSpec((1, d), lambda i: (0, 0)),
        ],
        out_specs=pl.BlockSpec((tm, d), lambda i: (i, 0)),
        out_shape=jax.ShapeDtypeStruct((m, d), BF16),
        compiler_params=_compiler_params(1),
        name="prenorm",
    )(x, shift, scale, g)


def _ffn_up_kernel(h_ref, w1_ref, w3_ref, act_ref):
    w1 = w1_ref[...].astype(BF16)
    w3 = w3_ref[...].astype(BF16)
    chunk = min(UP_ROW_CHUNK, h_ref.shape[0])
    for r in range(0, h_ref.shape[0], chunk):
        h = h_ref[r:r + chunk, :]
        a = _bdot(h, w1)
        b = _bdot(h, w3)
        act_ref[r:r + chunk, :] = (a * jax.nn.sigmoid(a) * b).astype(BF16)


def _ffn_up(h, w1, w3, seq):
    m, d = h.shape
    dff = w1.shape[1]
    tm = min(ROW_TILE, seq)
    tn = 256 if dff % 256 == 0 else dff
    return pl.pallas_call(
        _ffn_up_kernel,
        grid=(m // tm, dff // tn),
        in_specs=[
            pl.BlockSpec((tm, d), lambda i, j: (i, 0)),
            pl.BlockSpec((d, tn), lambda i, j: (0, j)),
            pl.BlockSpec((d, tn), lambda i, j: (0, j)),
        ],
        out_specs=pl.BlockSpec((tm, tn), lambda i, j: (i, j)),
        out_shape=jax.ShapeDtypeStruct((m, dff), BF16),
        compiler_params=_compiler_params(2),
        name="ffn_up",
    )(h, w1, w3)


def _down_kernel(act_ref, w_ref, x_ref, gate_ref, post_g_ref, *rest, n_k, coef, emit_h):
    if emit_h:
        shift_ref, scale_ref, g_ref, o_ref, h_ref, acc_ref = rest
    else:
        o_ref, acc_ref = rest
    k = pl.program_id(1)

    @pl.when(k == 0)
    def _():
        acc_ref[...] = _bdot(act_ref[...], w_ref[...].astype(BF16))

    @pl.when((k > 0) & (k < n_k))
    def _():
        acc_ref[...] += _bdot(act_ref[...], w_ref[...].astype(BF16))

    @pl.when(k >= n_k)
    def _():
        base = (k - n_k) * EPILOGUE_ROWS
        cg = coef * gate_ref[0]
        pg = post_g_ref[...]
        if emit_h:
            gs = g_ref[...] * (1.0 + scale_ref[0])
            shift = shift_ref[0]
        for r in range(0, o_ref.shape[0], NORM_ROWS):
            rows = slice(r, r + NORM_ROWS)
            f = acc_ref[pl.ds(pl.multiple_of(base + r, NORM_ROWS), NORM_ROWS), :]
            out = x_ref[rows, :] + cg * ((f * _inv_rms(f)) * pg)
            o_ref[rows, :] = out
            if emit_h:
                h_ref[rows, :] = _modulated_norm(out, gs, shift)


def _down(act, w, x, gate, post_g, next_mod, *, coef, tk, seq):
    m, d = x.shape
    kdim = act.shape[1]
    tm = min(ROW_TILE, seq)
    te = min(EPILOGUE_ROWS, tm)
    assert kdim % tk == 0 and tm % te == 0
    n_k = kdim // tk
    n_e = tm // te
    emit_h = next_mod is not None
    chunk_map = lambda i, k: (i * n_e + jnp.clip(k - n_k, 0, n_e - 1), 0)
    batch_vec = pl.BlockSpec((1, 1, d), lambda i, k: (i * tm // seq, 0, 0))
    row_vec = pl.BlockSpec((1, d), lambda i, k: (0, 0))
    in_specs = [
        pl.BlockSpec((tm, tk), lambda i, k: (i, jnp.minimum(k, n_k - 1))),
        pl.BlockSpec((tk, d), lambda i, k: (jnp.minimum(k, n_k - 1), 0)),
        pl.BlockSpec((te, d), chunk_map),
        batch_vec, row_vec,
    ]
    args = [act, w, x, gate, post_g]
    out_specs = [pl.BlockSpec((te, d), chunk_map)]
    out_shape = [jax.ShapeDtypeStruct((m, d), F32)]
    if emit_h:
        in_specs += [batch_vec, batch_vec, row_vec]
        args += list(next_mod)
        out_specs.append(pl.BlockSpec((te, d), chunk_map))
        out_shape.append(jax.ShapeDtypeStruct((m, d), BF16))
    kern = functools.partial(_down_kernel, n_k=n_k, coef=coef, emit_h=emit_h)
    res = pl.pallas_call(
        kern,
        grid=(m // tm, n_k + n_e),
        in_specs=in_specs,
        out_specs=out_specs,
        out_shape=out_shape,
        scratch_shapes=[pltpu.VMEM((tm, d), F32)],
        compiler_params=_compiler_params(2),
        name="down",
    )(*args)
    return (res[0], res[1]) if emit_h else (res[0], None)


def _proj_kernel(h_ref, w_ref, u_ref, qkv_ref, *, n_u_tiles, n_q_tiles, q_scale):
    j = pl.program_id(1)

    @pl.when(j < n_u_tiles)
    def _():
        u_ref[...] = _bdot(h_ref[...], w_ref[...].astype(BF16))

    @pl.when(j >= n_u_tiles)
    def _():
        factor = jnp.where(j < n_u_tiles + n_q_tiles, q_scale, 1.0).astype(F32)
        qkv_ref[...] = (_bdot(h_ref[...], w_ref[...].astype(BF16)) * factor).astype(BF16)


def _proj(h, w_in, d_pool, d_qk, head_dim, seq):
    m, d = h.shape
    d_in = w_in.shape[1]
    tn = min(512, d_pool)
    assert d_pool % tn == 0 and d_qk % tn == 0
    n_u = d_pool // tn
    tm = min(ROW_TILE, seq)
    kern = functools.partial(_proj_kernel, n_u_tiles=n_u, n_q_tiles=d_qk // tn,
                             q_scale=head_dim ** -0.5)
    return pl.pallas_call(
        kern,
        grid=(m // tm, d_in // tn),
        in_specs=[
            pl.BlockSpec((tm, d), lambda i, j: (i, 0), pipeline_mode=pl.Buffered(1)),
            pl.BlockSpec((d, tn), lambda i, j: (0, j)),
        ],
        out_specs=[
            pl.BlockSpec((tm, tn), lambda i, j: (i, jnp.minimum(j, n_u - 1))),
            pl.BlockSpec((tm, tn), lambda i, j: (i, jnp.maximum(j - n_u, 0))),
        ],
        out_shape=[
            jax.ShapeDtypeStruct((m, d_pool), F32),
            jax.ShapeDtypeStruct((m, d_in - d_pool), BF16),
        ],
        compiler_params=_compiler_params(2),
        name="proj",
    )(h, w_in)


def _pool_group(u_ref, w_ref, ps_ref, o_ref, group):
    win = POOL_WINDOWS[group]
    ug = u_ref[...]
    t = lax.broadcasted_iota(jnp.int32, ug.shape, 0)
    s = ug
    k = 1
    while k < win:
        s = s + jnp.where(t >= k, pltpu.roll(s, k, axis=0), 0.0)
        k *= 2
    count = jnp.minimum(t + 1, win).astype(F32)
    y = (s / count - ug).astype(BF16)
    o_ref[...] = (_bdot(y, w_ref[group].astype(BF16)) * ps_ref[...]).astype(BF16)


def _diff_attention(lam, sg_ref, q_ref, k_ref, v_ref, o_ref, lambda_init, tq):
    seq = q_ref.shape[0]
    hd = q_ref.shape[1] // 2
    sg = sg_ref[...]
    nt_dims = (((1,), (1,)), ((), ()))
    on_or_below_diag = (lax.broadcasted_iota(jnp.int32, (tq, tq), 1)
                        <= lax.broadcasted_iota(jnp.int32, (tq, tq), 0))
    n_blocks = seq // tq

    def scores(i):
        kv = (i + 1) * tq
        return [lax.dot_general(q_ref[i * tq:kv, cols], k_ref[0:kv, cols], nt_dims,
                                preferred_element_type=F32)
                for cols in (slice(0, hd), slice(hd, 2 * hd))]

    def exp_scores(i, s):
        kv = (i + 1) * tq
        diag = jnp.where(on_or_below_diag, s[:, kv - tq:], -jnp.inf)
        s = diag if i == 0 else jnp.concatenate([s[:, :kv - tq], diag], axis=1)
        p = jnp.exp(s - jnp.max(s, axis=-1, keepdims=True))
        return p, jnp.sum(p, axis=-1, keepdims=True)

    s_next = scores(0)
    for i in range(n_blocks):
        s_cur = s_next
        if i + 1 < n_blocks:
            s_next = scores(i + 1)
        kv = (i + 1) * tq
        p1, l1 = exp_scores(i, s_cur[0])
        p2, l2 = exp_scores(i, s_cur[1])
        a = p1 - p2 * (lam * l1 / l2)
        o = _bdot(a.astype(BF16), v_ref[0:kv, :]) * (1.0 / l1)
        o_ref[i * tq:kv, :] = (((o * _inv_rms(o)) * sg) * (1.0 - lambda_init)).astype(BF16)


def _mix_kernel(lq1_ref, lk1_ref, lq2_ref, lk2_ref, sg_ref, u_ref, wp_ref, ps_ref,
                q_ref, k_ref, v_ref, o_ref, *, lambda_init, tq):
    n_groups = len(POOL_WINDOWS)
    hh = pl.program_id(1)

    for g in range(n_groups):
        @pl.when(hh == g)
        def _(g=g):
            _pool_group(u_ref, wp_ref, ps_ref, o_ref, g)

    @pl.when(hh >= n_groups)
    def _():
        lam = (jnp.exp(jnp.sum(lq1_ref[...] * lk1_ref[...], axis=-1, keepdims=True))
               - jnp.exp(jnp.sum(lq2_ref[...] * lk2_ref[...], axis=-1, keepdims=True))
               + lambda_init)
        _diff_attention(lam, sg_ref, q_ref, k_ref, v_ref, o_ref, lambda_init, tq)


def _mix(u, qkv, w_pool, pool_scale, lq1, lk1, lq2, lk2, subln_g, n_heads, head_dim, seq,
         lambda_init):
    m, d_pool = u.shape
    n_groups = len(POOL_WINDOWS)
    hw = 2 * head_dim
    assert d_pool == n_groups * hw and w_pool.shape == (n_groups, hw, hw)
    vec = lambda n: pl.BlockSpec((1, n), lambda b, hh: (0, 0))
    head = lambda hh: jnp.maximum(hh - n_groups, 0)
    group = lambda hh: jnp.minimum(hh, n_groups - 1)
    kern = functools.partial(_mix_kernel, lambda_init=lambda_init, tq=min(256, seq))
    return pl.pallas_call(
        kern,
        grid=(m // seq, n_groups + n_heads),
        in_specs=[
            vec(head_dim), vec(head_dim), vec(head_dim), vec(head_dim), vec(hw),
            pl.BlockSpec((seq, hw), lambda b, hh: (b, group(hh))),
            pl.BlockSpec(w_pool.shape, lambda b, hh: (0, 0, 0)),
            pl.BlockSpec((1, hw), lambda b, hh: (0, group(hh))),
            pl.BlockSpec((seq, hw), lambda b, hh: (b, head(hh))),
            pl.BlockSpec((seq, hw), lambda b, hh: (b, n_heads + head(hh))),
            pl.BlockSpec((seq, hw), lambda b, hh: (b, 2 * n_heads + head(hh))),
        ],
        out_specs=pl.BlockSpec((seq, hw), lambda b, hh: (b, hh)),
        out_shape=jax.ShapeDtypeStruct((m, d_pool + n_heads * hw), BF16),
        compiler_params=_compiler_params(2),
        name="mix",
    )(lq1.reshape(1, -1), lk1.reshape(1, -1), lq2.reshape(1, -1), lk2.reshape(1, -1),
      subln_g.reshape(1, -1), u, w_pool, pool_scale.reshape(1, d_pool), qkv, qkv, qkv)


def kernel(x, c, w_ada, b_ada, pre_norm_g, post_norm_g, ffn1_w1, ffn1_w3, ffn1_w2, w_in, w_pool,
           pool_scale, lambda_q1, lambda_k1, lambda_q2, lambda_k2, subln_g, w_out, ffn2_w1,
           ffn2_w3, ffn2_w2):
    b, s, d = x.shape
    depth = w_ada.shape[0]
    d_pool = pool_scale.shape[1]
    head_dim = lambda_q1.shape[1]
    d_in = w_in.shape[2]
    d_qk = (d_in - d_pool) // 3
    n_heads = d_qk // (2 * head_dim)
    dff = ffn1_w1.shape[2]
    ffn_tk = 256 if dff % 256 == 0 else dff
    out_tk = min(256, d)
    xr = x.reshape(b * s, d)
    h = None
    for l in range(depth):
        lambda_init = 0.8 - 0.6 * math.exp(-0.3 * l)
        mod = _ada(c, w_ada[l], b_ada[l]).reshape(b, N_SUB, N_MOD, 1, d)
        shift, scale, gate = mod[:, :, 0], mod[:, :, 1], mod[:, :, 2]
        pre_g = pre_norm_g[l].reshape(N_SUB, 1, d)
        post_g = post_norm_g[l].reshape(N_SUB, 1, d)
        sub_mod = lambda i: (shift[:, i], scale[:, i], pre_g[i])

        if h is None:
            h = _prenorm(xr, *sub_mod(0), s)
        act = _ffn_up(h, ffn1_w1[l], ffn1_w3[l], s)
        xr, h = _down(act, ffn1_w2[l], xr, gate[:, 0], post_g[0], sub_mod(1),
                      coef=0.5, tk=ffn_tk, seq=s)

        u, qkv = _proj(h, w_in[l], d_pool, d_qk, head_dim, s)
        mixed = _mix(u, qkv, w_pool[l], pool_scale[l], lambda_q1[l], lambda_k1[l], lambda_q2[l],
                     lambda_k2[l], subln_g[l], n_heads, head_dim, s, lambda_init)
        xr, h = _down(mixed, w_out[l], xr, gate[:, 1], post_g[1], sub_mod(2),
                      coef=1.0, tk=out_tk, seq=s)

        act = _ffn_up(h, ffn2_w1[l], ffn2_w3[l], s)
        xr, h = _down(act, ffn2_w2[l], xr, gate[:, 2], post_g[2], None,
                      coef=0.5, tk=ffn_tk, seq=s)
    return xr.reshape(b, s, d)
```

```python
import functools
import math

import jax
import jax.numpy as jnp
from jax import lax
from jax.experimental import pallas as pl
from jax.experimental.pallas import tpu as pltpu

F32 = jnp.float32
BF16 = jnp.bfloat16

NORM_EPS = 1e-6
POOL_WINDOWS = (2, 4, 8, 16)
N_SUB = 3
N_MOD = 3
V7X_VMEM_LIMIT_BYTES = 58 * 1024 * 1024
NORM_ROWS = 32
ROW_TILE = 2048
EPILOGUE_ROWS = 128
UP_ROW_CHUNK = 512
X_SLOTS = 3


def _compiler_params(n_grid_dims):
    return pltpu.CompilerParams(
        dimension_semantics=("arbitrary",) * n_grid_dims,
        vmem_limit_bytes=V7X_VMEM_LIMIT_BYTES,
    )


def _inv_rms(xf):
    return lax.rsqrt(jnp.mean(xf * xf, axis=-1, keepdims=True) + NORM_EPS)


def _bdot(a, b):
    return jnp.dot(a, b, preferred_element_type=F32)


def _ada_kernel(c_ref, w_ref, b_ref, o_ref):
    c = c_ref[...]
    act = (c * jax.nn.sigmoid(c)).astype(BF16)
    o_ref[...] = _bdot(act, w_ref[...].astype(BF16)) + b_ref[...]


def _ada(c, w_ada, b_ada):
    b, d = c.shape
    n = w_ada.shape[1]
    rows = 8
    tn = 512 if n % 512 == 0 else n
    c_pad = jnp.pad(c, ((0, rows - b), (0, 0)))
    out = pl.pallas_call(
        _ada_kernel,
        grid=(n // tn,),
        in_specs=[
            pl.BlockSpec((rows, d), lambda j: (0, 0)),
            pl.BlockSpec((d, tn), lambda j: (0, j)),
            pl.BlockSpec((1, tn), lambda j: (0, j)),
        ],
        out_specs=pl.BlockSpec((rows, tn), lambda j: (0, j)),
        out_shape=jax.ShapeDtypeStruct((rows, n), F32),
        compiler_params=_compiler_params(1),
        name="ada",
    )(c_pad, w_ada, b_ada.reshape(1, n))
    return out[:b]


def _modulated_norm(xf, gs, shift):
    return ((xf * _inv_rms(xf)) * gs + shift).astype(BF16)


def _prenorm_kernel(x_ref, shift_ref, scale_ref, g_ref, h_ref):
    gs = g_ref[...] * (1.0 + scale_ref[0])
    shift = shift_ref[0]
    for r in range(0, x_ref.shape[0], NORM_ROWS):
        rows = slice(r, r + NORM_ROWS)
        h_ref[rows, :] = _modulated_norm(x_ref[rows, :], gs, shift)


def _prenorm(x, shift, scale, g, seq):
    m, d = x.shape
    tm = min(256, seq)
    return pl.pallas_call(
        _prenorm_kernel,
        grid=(m // tm,),
        in_specs=[
            pl.BlockSpec((tm, d), lambda i: (i, 0)),
            pl.BlockSpec((1, 1, d), lambda i: (i * tm // seq, 0, 0)),
            pl.BlockSpec((1, 1, d), lambda i: (i * tm // seq, 0, 0)),
            pl.BlockSpec((1, d), lambda i: (0, 0)),
        ],
        out_specs=pl.BlockSpec((tm, d), lambda i: (i, 0)),
        out_shape=jax.ShapeDtypeStruct((m, d), BF16),
        compiler_params=_compiler_params(1),
        name="prenorm",
    )(x, shift, scale, g)


def _ffn_up_kernel(h_ref, w1_ref, w3_ref, act_ref):
    w1 = w1_ref[...].astype(BF16)
    w3 = w3_ref[...].astype(BF16)
    chunk = min(UP_ROW_CHUNK, h_ref.shape[0])
    for r in range(0, h_ref.shape[0], chunk):
        h = h_ref[r:r + chunk, :]
        a = _bdot(h, w1)
        b = _bdot(h, w3)
        act_ref[r:r + chunk, :] = (a * jax.nn.sigmoid(a) * b).astype(BF16)


def _ffn_up(h, w1, w3, seq):
    m, d = h.shape
    dff = w1.shape[1]
    tm = min(ROW_TILE, seq)
    tn = 256 if dff % 256 == 0 else dff
    return pl.pallas_call(
        _ffn_up_kernel,
        grid=(m // tm, dff // tn),
        in_specs=[
            pl.BlockSpec((tm, d), lambda i, j: (i, 0)),
            pl.BlockSpec((d, tn), lambda i, j: (0, j)),
            pl.BlockSpec((d, tn), lambda i, j: (0, j)),
        ],
        out_specs=pl.BlockSpec((tm, tn), lambda i, j: (i, j)),
        out_shape=jax.ShapeDtypeStruct((m, dff), BF16),
        compiler_params=_compiler_params(2),
        name="ffn_up",
    )(h, w1, w3)


def _down_kernel(act_ref, w_ref, gate_ref, post_g_ref, *rest, n_k, coef, emit_h):
    if emit_h:
        (shift_ref, scale_ref, g_ref, x_hbm, o_hbm, h_hbm,
         acc_ref, xbuf, obuf, hbuf, sem_x, sem_o, sem_h) = rest
    else:
        x_hbm, o_hbm, acc_ref, xbuf, obuf, sem_x, sem_o = rest
    k = pl.program_id(1)
    tm = acc_ref.shape[0]
    te = xbuf.shape[1]
    n_chunks = tm // te
    row0 = pl.program_id(0) * tm

    def chunk_rows(c):
        return pl.ds(pl.multiple_of(row0 + c * te, te), te)

    def x_copy(c, slot):
        return pltpu.make_async_copy(x_hbm.at[chunk_rows(c), :], xbuf.at[slot], sem_x.at[slot])

    def o_copy(c, slot):
        return pltpu.make_async_copy(obuf.at[slot], o_hbm.at[chunk_rows(c), :], sem_o.at[slot])

    def h_copy(c, slot):
        return pltpu.make_async_copy(hbuf.at[slot], h_hbm.at[chunk_rows(c), :], sem_h.at[slot])

    def wait_results(c, slot):
        o_copy(c, slot).wait()
        if emit_h:
            h_copy(c, slot).wait()

    @pl.when(k == n_k - 1)
    def _():
        for c in range(min(X_SLOTS, n_chunks)):
            x_copy(c, c).start()

    @pl.when(k == 0)
    def _():
        acc_ref[...] = _bdot(act_ref[...], w_ref[...].astype(BF16))

    @pl.when((k > 0) & (k < n_k))
    def _():
        acc_ref[...] += _bdot(act_ref[...], w_ref[...].astype(BF16))

    @pl.when(k == n_k)
    def _():
        cg = coef * gate_ref[0]
        pg = post_g_ref[...]
        if emit_h:
            gs = g_ref[...] * (1.0 + scale_ref[0])
            shift = shift_ref[0]

        def chunk(c, carry):
            xs = c % X_SLOTS
            rs = c % 2
            x_copy(c, xs).wait()

            @pl.when(c >= 2)
            def _():
                wait_results(c - 2, rs)

            base = c * te
            for r in range(0, te, NORM_ROWS):
                rows = slice(r, r + NORM_ROWS)
                f = acc_ref[pl.ds(pl.multiple_of(base + r, NORM_ROWS), NORM_ROWS), :]
                out = xbuf[xs, rows, :] + cg * ((f * _inv_rms(f)) * pg)
                obuf[rs, rows, :] = out
                if emit_h:
                    hbuf[rs, rows, :] = _modulated_norm(out, gs, shift)
            o_copy(c, rs).start()
            if emit_h:
                h_copy(c, rs).start()

            @pl.when(c + X_SLOTS < n_chunks)
            def _():
                x_copy(c + X_SLOTS, xs).start()

            return carry

        lax.fori_loop(0, n_chunks, chunk, 0)
        for c in range(max(n_chunks - 2, 0), n_chunks):
            wait_results(c, c % 2)


def _down(act, w, x, gate, post_g, next_mod, *, coef, tk, seq):
    m, d = x.shape
    kdim = act.shape[1]
    tm = min(ROW_TILE, seq)
    te = min(EPILOGUE_ROWS, tm)
    assert kdim % tk == 0 and tm % te == 0
    n_k = kdim // tk
    emit_h = next_mod is not None
    k_tile = lambda k: jnp.minimum(k, n_k - 1)
    batch_vec = pl.BlockSpec((1, 1, d), lambda i, k: (i * tm // seq, 0, 0))
    row_vec = pl.BlockSpec((1, d), lambda i, k: (0, 0))
    hbm = pl.BlockSpec(memory_space=pl.ANY)
    in_specs = [
        pl.BlockSpec((tm, tk), lambda i, k: (i, k_tile(k))),
        pl.BlockSpec((tk, d), lambda i, k: (k_tile(k), 0)),
        batch_vec, row_vec,
    ]
    args = [act, w, gate, post_g]
    if emit_h:
        in_specs += [batch_vec, batch_vec, row_vec]
        args += list(next_mod)
    in_specs.append(hbm)
    args.append(x)
    out_specs = [hbm]
    out_shape = [jax.ShapeDtypeStruct((m, d), F32)]
    scratch = [pltpu.VMEM((tm, d), F32), pltpu.VMEM((X_SLOTS, te, d), F32),
               pltpu.VMEM((2, te, d), F32)]
    sems = [pltpu.SemaphoreType.DMA((X_SLOTS,)), pltpu.SemaphoreType.DMA((2,))]
    if emit_h:
        out_specs.append(hbm)
        out_shape.append(jax.ShapeDtypeStruct((m, d), BF16))
        scratch.append(pltpu.VMEM((2, te, d), BF16))
        sems.append(pltpu.SemaphoreType.DMA((2,)))
    kern = functools.partial(_down_kernel, n_k=n_k, coef=coef, emit_h=emit_h)
    res = pl.pallas_call(
        kern,
        grid=(m // tm, n_k + 1),
        in_specs=in_specs,
        out_specs=out_specs,
        out_shape=out_shape,
        scratch_shapes=scratch + sems,
        compiler_params=_compiler_params(2),
        name="down",
    )(*args)
    return (res[0], res[1]) if emit_h else (res[0], None)


def _proj_kernel(h_ref, w_ref, u_ref, qkv_ref, *, n_u_tiles, n_q_tiles, q_scale):
    j = pl.program_id(1)

    @pl.when(j < n_u_tiles)
    def _():
        u_ref[...] = _bdot(h_ref[...], w_ref[...].astype(BF16))

    @pl.when(j >= n_u_tiles)
    def _():
        factor = jnp.where(j < n_u_tiles + n_q_tiles, q_scale, 1.0).astype(F32)
        qkv_ref[...] = (_bdot(h_ref[...], w_ref[...].astype(BF16)) * factor).astype(BF16)


def _proj(h, w_in, d_pool, d_qk, head_dim, seq):
    m, d = h.shape
    d_in = w_in.shape[1]
    tn = min(512, d_pool)
    assert d_pool % tn == 0 and d_qk % tn == 0
    n_u = d_pool // tn
    tm = min(ROW_TILE, seq)
    kern = functools.partial(_proj_kernel, n_u_tiles=n_u, n_q_tiles=d_qk // tn,
                             q_scale=head_dim ** -0.5)
    return pl.pallas_call(
        kern,
        grid=(m // tm, d_in // tn),
        in_specs=[
            pl.BlockSpec((tm, d), lambda i, j: (i, 0), pipeline_mode=pl.Buffered(1)),
            pl.BlockSpec((d, tn), lambda i, j: (0, j)),
        ],
        out_specs=[
            pl.BlockSpec((tm, tn), lambda i, j: (i, jnp.minimum(j, n_u - 1))),
            pl.BlockSpec((tm, tn), lambda i, j: (i, jnp.maximum(j - n_u, 0))),
        ],
        out_shape=[
            jax.ShapeDtypeStruct((m, d_pool), F32),
            jax.ShapeDtypeStruct((m, d_in - d_pool), BF16),
        ],
        compiler_params=_compiler_params(2),
        name="proj",
    )(h, w_in)


def _pool_group(u_ref, w_ref, ps_ref, o_ref, group):
    win = POOL_WINDOWS[group]
    ug = u_ref[...]
    t = lax.broadcasted_iota(jnp.int32, ug.shape, 0)
    s = ug
    k = 1
    while k < win:
        s = s + jnp.where(t >= k, pltpu.roll(s, k, axis=0), 0.0)
        k *= 2
    count = jnp.minimum(t + 1, win).astype(F32)
    y = (s / count - ug).astype(BF16)
    o_ref[...] = (_bdot(y, w_ref[group].astype(BF16)) * ps_ref[...]).astype(BF16)


def _diff_attention(lam, sg_ref, q_ref, k_ref, v_ref, o_ref, lambda_init, tq):
    seq = q_ref.shape[0]
    hd = q_ref.shape[1] // 2
    sg = sg_ref[...]
    nt_dims = (((1,), (1,)), ((), ()))
    on_or_below_diag = (lax.broadcasted_iota(jnp.int32, (tq, tq), 1)
                        <= lax.broadcasted_iota(jnp.int32, (tq, tq), 0))
    n_blocks = seq // tq

    def scores(i):
        kv = (i + 1) * tq
        return [lax.dot_general(q_ref[i * tq:kv, cols], k_ref[0:kv, cols], nt_dims,
                                preferred_element_type=F32)
                for cols in (slice(0, hd), slice(hd, 2 * hd))]

    def exp_scores(i, s):
        kv = (i + 1) * tq
        diag = jnp.where(on_or_below_diag, s[:, kv - tq:], -jnp.inf)
        s = diag if i == 0 else jnp.concatenate([s[:, :kv - tq], diag], axis=1)
        p = jnp.exp(s - jnp.max(s, axis=-1, keepdims=True))
        return p, jnp.sum(p, axis=-1, keepdims=True)

    s_next = scores(0)
    for i in range(n_blocks):
        s_cur = s_next
        if i + 1 < n_blocks:
            s_next = scores(i + 1)
        kv = (i + 1) * tq
        p1, l1 = exp_scores(i, s_cur[0])
        p2, l2 = exp_scores(i, s_cur[1])
        a = p1 - p2 * (lam * l1 / l2)
        o = _bdot(a.astype(BF16), v_ref[0:kv, :]) * (1.0 / l1)
        o_ref[i * tq:kv, :] = (((o * _inv_rms(o)) * sg) * (1.0 - lambda_init)).astype(BF16)


def _mix_kernel(lq1_ref, lk1_ref, lq2_ref, lk2_ref, sg_ref, u_ref, wp_ref, ps_ref,
                q_ref, k_ref, v_ref, o_ref, *, lambda_init, tq):
    n_groups = len(POOL_WINDOWS)
    hh = pl.program_id(1)

    for g in range(n_groups):
        @pl.when(hh == g)
        def _(g=g):
            _pool_group(u_ref, wp_ref, ps_ref, o_ref, g)

    @pl.when(hh >= n_groups)
    def _():
        lam = (jnp.exp(jnp.sum(lq1_ref[...] * lk1_ref[...], axis=-1, keepdims=True))
               - jnp.exp(jnp.sum(lq2_ref[...] * lk2_ref[...], axis=-1, keepdims=True))
               + lambda_init)
        _diff_attention(lam, sg_ref, q_ref, k_ref, v_ref, o_ref, lambda_init, tq)


def _mix(u, qkv, w_pool, pool_scale, lq1, lk1, lq2, lk2, subln_g, n_heads, head_dim, seq,
         lambda_init):
    m, d_pool = u.shape
    n_groups = len(POOL_WINDOWS)
    hw = 2 * head_dim
    assert d_pool == n_groups * hw and w_pool.shape == (n_groups, hw, hw)
    vec = lambda n: pl.BlockSpec((1, n), lambda b, hh: (0, 0))
    head = lambda hh: jnp.maximum(hh - n_groups, 0)
    group = lambda hh: jnp.minimum(hh, n_groups - 1)
    kern = functools.partial(_mix_kernel, lambda_init=lambda_init, tq=min(256, seq))
    return pl.pallas_call(
        kern,
        grid=(m // seq, n_groups + n_heads),
        in_specs=[
            vec(head_dim), vec(head_dim), vec(head_dim), vec(head_dim), vec(hw),
            pl.BlockSpec((seq, hw), lambda b, hh: (b, group(hh))),
            pl.BlockSpec(w_pool.shape, lambda b, hh: (0, 0, 0)),
            pl.BlockSpec((1, hw), lambda b, hh: (0, group(hh))),
            pl.BlockSpec((seq, hw), lambda b, hh: (b, head(hh))),
            pl.BlockSpec((seq, hw), lambda b, hh: (b, n_heads + head(hh))),
            pl.BlockSpec((seq, hw), lambda b, hh: (b, 2 * n_heads + head(hh))),
        ],
        out_specs=pl.BlockSpec((seq, hw), lambda b, hh: (b, hh)),
        out_shape=jax.ShapeDtypeStruct((m, d_pool + n_heads * hw), BF16),
        compiler_params=_compiler_params(2),
        name="mix",
    )(lq1.reshape(1, -1), lk1.reshape(1, -1), lq2.reshape(1, -1), lk2.reshape(1, -1),
      subln_g.reshape(1, -1), u, w_pool, pool_scale.reshape(1, d_pool), qkv, qkv, qkv)


def kernel(x, c, w_ada, b_ada, pre_norm_g, post_norm_g, ffn1_w1, ffn1_w3, ffn1_w2, w_in, w_pool,
           pool_scale, lambda_q1, lambda_k1, lambda_q2, lambda_k2, subln_g, w_out, ffn2_w1,
           ffn2_w3, ffn2_w2):
    b, s, d = x.shape
    depth = w_ada.shape[0]
    d_pool = pool_scale.shape[1]
    head_dim = lambda_q1.shape[1]
    d_in = w_in.shape[2]
    d_qk = (d_in - d_pool) // 3
    n_heads = d_qk // (2 * head_dim)
    dff = ffn1_w1.shape[2]
    ffn_tk = 256 if dff % 256 == 0 else dff
    out_tk = min(256, d)
    xr = x.reshape(b * s, d)
    h = None
    for l in range(depth):
        lambda_init = 0.8 - 0.6 * math.exp(-0.3 * l)
        mod = _ada(c, w_ada[l], b_ada[l]).reshape(b, N_SUB, N_MOD, 1, d)
        shift, scale, gate = mod[:, :, 0], mod[:, :, 1], mod[:, :, 2]
        pre_g = pre_norm_g[l].reshape(N_SUB, 1, d)
        post_g = post_norm_g[l].reshape(N_SUB, 1, d)
        sub_mod = lambda i: (shift[:, i], scale[:, i], pre_g[i])

        if h is None:
            h = _prenorm(xr, *sub_mod(0), s)
        act = _ffn_up(h, ffn1_w1[l], ffn1_w3[l], s)
        xr, h = _down(act, ffn1_w2[l], xr, gate[:, 0], post_g[0], sub_mod(1),
                      coef=0.5, tk=ffn_tk, seq=s)

        u, qkv = _proj(h, w_in[l], d_pool, d_qk, head_dim, s)
        mixed = _mix(u, qkv, w_pool[l], pool_scale[l], lambda_q1[l], lambda_k1[l], lambda_q2[l],
                     lambda_k2[l], subln_g[l], n_heads, head_dim, s, lambda_init)
        xr, h = _down(mixed, w_out[l], xr, gate[:, 1], post_g[1], sub_mod(2),
                      coef=1.0, tk=out_tk, seq=s)

        act = _ffn_up(h, ffn2_w1[l], ffn2_w3[l], s)
        xr, h = _down(act, ffn2_w2[l], xr, gate[:, 2], post_g[2], None,
                      coef=0.5, tk=ffn_tk, seq=s)
    return xr.reshape(b, s, d)
```

```python
import functools
import math

import jax
import jax.numpy as jnp
from jax import lax
from jax.experimental import pallas as pl
from jax.experimental.pallas import tpu as pltpu

F32 = jnp.float32
BF16 = jnp.bfloat16

NORM_EPS = 1e-6
POOL_WINDOWS = (2, 4, 8, 16)
N_SUB = 3
N_EARLY_MOD = 5
V7X_VMEM_LIMIT_BYTES = 58 * 1024 * 1024
NORM_ROWS = 8
ROW_TILE = 2048
EPILOGUE_ROWS = 128
UP_ROW_CHUNK = 512
X_SLOTS = 3


def _compiler_params(n_grid_dims):
    return pltpu.CompilerParams(
        dimension_semantics=("arbitrary",) * n_grid_dims,
        vmem_limit_bytes=V7X_VMEM_LIMIT_BYTES,
    )


def _inv_rms(xf):
    return lax.rsqrt(jnp.mean(xf * xf, axis=-1, keepdims=True) + NORM_EPS)


def _bdot(a, b):
    return jnp.dot(a, b, preferred_element_type=F32)


def _ada_block(c_ref, w_ref, b_ref, o_ref):
    c = c_ref[...]
    act = (c * jax.nn.sigmoid(c)).astype(BF16)
    o_ref[...] = _bdot(act, w_ref[...].astype(BF16)) + b_ref[...]


def _ada(c_pad, w_ada, b_ada, n_cols):
    rows, d = c_pad.shape
    tn = 512 if n_cols % 512 == 0 else n_cols
    return pl.pallas_call(
        _ada_block,
        grid=(n_cols // tn,),
        in_specs=[
            pl.BlockSpec((rows, d), lambda j: (0, 0)),
            pl.BlockSpec((d, tn), lambda j: (0, j)),
            pl.BlockSpec((1, tn), lambda j: (0, j)),
        ],
        out_specs=pl.BlockSpec((rows, tn), lambda j: (0, j)),
        out_shape=jax.ShapeDtypeStruct((rows, n_cols), F32),
        compiler_params=_compiler_params(1),
        name="ada",
    )(c_pad, w_ada, b_ada)


def _modulated_norm(xf, gs, shift):
    return ((xf * _inv_rms(xf)) * gs + shift).astype(BF16)


def _prenorm_kernel(x_ref, shift_ref, scale_ref, g_ref, h_ref):
    gs = g_ref[...] * (1.0 + scale_ref[0])
    shift = shift_ref[0]
    for r in range(0, x_ref.shape[0], NORM_ROWS):
        rows = slice(r, r + NORM_ROWS)
        h_ref[rows, :] = _modulated_norm(x_ref[rows, :], gs, shift)


def _prenorm(x, shift, scale, g, seq):
    m, d = x.shape
    tm = min(256, seq)
    return pl.pallas_call(
        _prenorm_kernel,
        grid=(m // tm,),
        in_specs=[
            pl.BlockSpec((tm, d), lambda i: (i, 0)),
            pl.BlockSpec((1, 1, d), lambda i: (i * tm // seq, 0, 0)),
            pl.BlockSpec((1, 1, d), lambda i: (i * tm // seq, 0, 0)),
            pl.BlockSpec((1, d), lambda i: (0, 0)),
        ],
        out_specs=pl.BlockSpec((tm, d), lambda i: (i, 0)),
        out_shape=jax.ShapeDtypeStruct((m, d), BF16),
        compiler_params=_compiler_params(1),
        name="prenorm",
    )(x, shift, scale, g)


def _ffn_up_kernel(h_ref, w1_ref, w3_ref, act_ref):
    w1 = w1_ref[...].astype(BF16)
    w3 = w3_ref[...].astype(BF16)
    chunk = min(UP_ROW_CHUNK, h_ref.shape[0])
    for r in range(0, h_ref.shape[0], chunk):
        h = h_ref[r:r + chunk, :]
        a = _bdot(h, w1)
        b = _bdot(h, w3)
        act_ref[r:r + chunk, :] = (a * jax.nn.sigmoid(a) * b).astype(BF16)


def _ffn_up(h, w1, w3, seq):
    m, d = h.shape
    dff = w1.shape[1]
    tm = min(ROW_TILE, seq)
    tn = 256 if dff % 256 == 0 else dff
    return pl.pallas_call(
        _ffn_up_kernel,
        grid=(m // tm, dff // tn),
        in_specs=[
            pl.BlockSpec((tm, d), lambda i, j: (i, 0)),
            pl.BlockSpec((d, tn), lambda i, j: (0, j)),
            pl.BlockSpec((d, tn), lambda i, j: (0, j)),
        ],
        out_specs=pl.BlockSpec((tm, tn), lambda i, j: (i, j)),
        out_shape=jax.ShapeDtypeStruct((m, dff), BF16),
        compiler_params=_compiler_params(2),
        name="ffn_up",
    )(h, w1, w3)


def _down_kernel(act_ref, w_ref, gate_ref, post_g_ref, *rest, n_k, coef, emit_h):
    if emit_h:
        (shift_ref, scale_ref, g_ref, x_hbm, o_hbm, h_hbm,
         acc_ref, xbuf, obuf, hbuf, sem_x, sem_o, sem_h) = rest
    else:
        x_hbm, o_hbm, acc_ref, xbuf, obuf, sem_x, sem_o = rest
    k = pl.program_id(1)
    tm = acc_ref.shape[0]
    te = xbuf.shape[1]
    n_chunks = tm // te
    row0 = pl.program_id(0) * tm

    def chunk_rows(c):
        return pl.ds(pl.multiple_of(row0 + c * te, te), te)

    def x_copy(c, slot):
        return pltpu.make_async_copy(x_hbm.at[chunk_rows(c), :], xbuf.at[slot], sem_x.at[slot])

    def o_copy(c, slot):
        return pltpu.make_async_copy(obuf.at[slot], o_hbm.at[chunk_rows(c), :], sem_o.at[slot])

    def h_copy(c, slot):
        return pltpu.make_async_copy(hbuf.at[slot], h_hbm.at[chunk_rows(c), :], sem_h.at[slot])

    def wait_results(c, slot):
        o_copy(c, slot).wait()
        if emit_h:
            h_copy(c, slot).wait()

    @pl.when(k == n_k - 1)
    def _():
        for c in range(min(X_SLOTS, n_chunks)):
            x_copy(c, c).start()

    @pl.when(k == 0)
    def _():
        acc_ref[...] = _bdot(act_ref[...], w_ref[...].astype(BF16))

    @pl.when((k > 0) & (k < n_k))
    def _():
        acc_ref[...] += _bdot(act_ref[...], w_ref[...].astype(BF16))

    @pl.when(k == n_k)
    def _():
        cg = coef * gate_ref[0]
        pg = post_g_ref[...]
        if emit_h:
            gs = g_ref[...] * (1.0 + scale_ref[0])
            shift = shift_ref[0]

        def chunk(c, carry):
            xs = c % X_SLOTS
            rs = c % 2
            x_copy(c, xs).wait()

            @pl.when(c >= 2)
            def _():
                wait_results(c - 2, rs)

            base = c * te
            for r in range(0, te, NORM_ROWS):
                rows = slice(r, r + NORM_ROWS)
                f = acc_ref[pl.ds(pl.multiple_of(base + r, NORM_ROWS), NORM_ROWS), :]
                out = xbuf[xs, rows, :] + cg * ((f * _inv_rms(f)) * pg)
                obuf[rs, rows, :] = out
                if emit_h:
                    hbuf[rs, rows, :] = _modulated_norm(out, gs, shift)
            o_copy(c, rs).start()
            if emit_h:
                h_copy(c, rs).start()

            @pl.when(c + X_SLOTS < n_chunks)
            def _():
                x_copy(c + X_SLOTS, xs).start()

            return carry

        lax.fori_loop(0, n_chunks, chunk, 0)
        for c in range(max(n_chunks - 2, 0), n_chunks):
            wait_results(c, c % 2)


def _down(act, w, x, gate, post_g, next_mod, *, coef, tk, seq):
    m, d = x.shape
    kdim = act.shape[1]
    tm = min(ROW_TILE, seq)
    te = min(EPILOGUE_ROWS, tm)
    assert kdim % tk == 0 and tm % te == 0
    n_k = kdim // tk
    emit_h = next_mod is not None
    k_tile = lambda k: jnp.minimum(k, n_k - 1)
    batch_vec = pl.BlockSpec((1, 1, d), lambda i, k: (i * tm // seq, 0, 0))
    row_vec = pl.BlockSpec((1, d), lambda i, k: (0, 0))
    hbm = pl.BlockSpec(memory_space=pl.ANY)
    in_specs = [
        pl.BlockSpec((tm, tk), lambda i, k: (i, k_tile(k))),
        pl.BlockSpec((tk, d), lambda i, k: (k_tile(k), 0)),
        batch_vec, row_vec,
    ]
    args = [act, w, gate, post_g]
    if emit_h:
        in_specs += [batch_vec, batch_vec, row_vec]
        args += list(next_mod)
    in_specs.append(hbm)
    args.append(x)
    out_specs = [hbm]
    out_shape = [jax.ShapeDtypeStruct((m, d), F32)]
    scratch = [pltpu.VMEM((tm, d), F32), pltpu.VMEM((X_SLOTS, te, d), F32),
               pltpu.VMEM((2, te, d), F32)]
    sems = [pltpu.SemaphoreType.DMA((X_SLOTS,)), pltpu.SemaphoreType.DMA((2,))]
    if emit_h:
        out_specs.append(hbm)
        out_shape.append(jax.ShapeDtypeStruct((m, d), BF16))
        scratch.append(pltpu.VMEM((2, te, d), BF16))
        sems.append(pltpu.SemaphoreType.DMA((2,)))
    kern = functools.partial(_down_kernel, n_k=n_k, coef=coef, emit_h=emit_h)
    res = pl.pallas_call(
        kern,
        grid=(m // tm, n_k + 1),
        in_specs=in_specs,
        out_specs=out_specs,
        out_shape=out_shape,
        scratch_shapes=scratch + sems,
        compiler_params=_compiler_params(2),
        name="down",
    )(*args)
    return (res[0], res[1]) if emit_h else (res[0], None)


def _proj_kernel(h_ref, w_ref, u_ref, qkv_ref, *, n_u_tiles, n_q_tiles, q_scale):
    j = pl.program_id(1)

    @pl.when(j < n_u_tiles)
    def _():
        u_ref[...] = _bdot(h_ref[...], w_ref[...].astype(BF16))

    @pl.when(j >= n_u_tiles)
    def _():
        factor = jnp.where(j < n_u_tiles + n_q_tiles, q_scale, 1.0).astype(F32)
        qkv_ref[...] = (_bdot(h_ref[...], w_ref[...].astype(BF16)) * factor).astype(BF16)


def _proj(h, w_in, d_pool, d_qk, head_dim, seq):
    m, d = h.shape
    d_in = w_in.shape[1]
    tn = min(512, d_pool)
    assert d_pool % tn == 0 and d_qk % tn == 0
    n_u = d_pool // tn
    tm = min(ROW_TILE, seq)
    kern = functools.partial(_proj_kernel, n_u_tiles=n_u, n_q_tiles=d_qk // tn,
                             q_scale=math.log2(math.e) * head_dim ** -0.5)
    return pl.pallas_call(
        kern,
        grid=(m // tm, d_in // tn),
        in_specs=[
            pl.BlockSpec((tm, d), lambda i, j: (i, 0), pipeline_mode=pl.Buffered(1)),
            pl.BlockSpec((d, tn), lambda i, j: (0, j)),
        ],
        out_specs=[
            pl.BlockSpec((tm, tn), lambda i, j: (i, jnp.minimum(j, n_u - 1))),
            pl.BlockSpec((tm, tn), lambda i, j: (i, jnp.maximum(j - n_u, 0))),
        ],
        out_shape=[
            jax.ShapeDtypeStruct((m, d_pool), F32),
            jax.ShapeDtypeStruct((m, d_in - d_pool), BF16),
        ],
        compiler_params=_compiler_params(2),
        name="proj",
    )(h, w_in)


def _pool_group(u_ref, w_ref, ps_ref, o_ref, group):
    win = POOL_WINDOWS[group]
    ug = u_ref[...]
    t = lax.broadcasted_iota(jnp.int32, ug.shape, 0)
    s = ug
    k = 1
    while k < win:
        s = s + jnp.where(t >= k, pltpu.roll(s, k, axis=0), 0.0)
        k *= 2
    count = jnp.minimum(t + 1, win).astype(F32)
    y = (s / count - ug).astype(BF16)
    o_ref[...] = (_bdot(y, w_ref[group].astype(BF16)) * ps_ref[...]).astype(BF16)


def _diff_attention(lam, sg_ref, q_ref, k_ref, v_ref, o_ref, lambda_init, tq):
    seq = q_ref.shape[0]
    hd = q_ref.shape[1] // 2
    sg = sg_ref[...]
    nt_dims = (((1,), (1,)), ((), ()))
    on_or_below_diag = (lax.broadcasted_iota(jnp.int32, (tq, tq), 1)
                        <= lax.broadcasted_iota(jnp.int32, (tq, tq), 0))
    n_blocks = seq // tq

    def scores(i):
        kv = (i + 1) * tq
        return [lax.dot_general(q_ref[i * tq:kv, cols], k_ref[0:kv, cols], nt_dims,
                                preferred_element_type=F32)
                for cols in (slice(0, hd), slice(hd, 2 * hd))]

    def exp_scores(i, s):
        kv = (i + 1) * tq
        diag = jnp.where(on_or_below_diag, s[:, kv - tq:], -jnp.inf)
        s = diag if i == 0 else jnp.concatenate([s[:, :kv - tq], diag], axis=1)
        p = jnp.exp2(s - jnp.max(s, axis=-1, keepdims=True))
        return p, jnp.sum(p, axis=-1, keepdims=True)

    def weights(i, s):
        p1, l1 = exp_scores(i, s[0])
        p2, l2 = exp_scores(i, s[1])
        return (p1 - p2 * (lam * l1 / l2)).astype(BF16), 1.0 / l1

    def finish(i, a, inv_l1):
        kv = (i + 1) * tq
        o = _bdot(a, v_ref[0:kv, :]) * inv_l1
        o_ref[i * tq:kv, :] = (((o * _inv_rms(o)) * sg) * (1.0 - lambda_init)).astype(BF16)

    s_next = scores(0)
    pending = None
    for i in range(n_blocks):
        s_cur = s_next
        if i + 1 < n_blocks:
            s_next = scores(i + 1)
        cur = (i,) + weights(i, s_cur)
        if pending is not None:
            finish(*pending)
        pending = cur
    finish(*pending)


def _mix_kernel(lq1_ref, lk1_ref, lq2_ref, lk2_ref, sg_ref, u_ref, wp_ref, ps_ref,
                q_ref, k_ref, v_ref, c_ref, wa_ref, ba_ref, o_ref, mod_ref, *, lambda_init, tq):
    n_groups = len(POOL_WINDOWS)
    hh = pl.program_id(1)

    for g in range(n_groups):
        @pl.when(hh == g)
        def _(g=g):
            _ada_block(c_ref, wa_ref, ba_ref, mod_ref)
            _pool_group(u_ref, wp_ref, ps_ref, o_ref, g)

    @pl.when(hh >= n_groups)
    def _():
        _ada_block(c_ref, wa_ref, ba_ref, mod_ref)
        lam = (jnp.exp(jnp.sum(lq1_ref[...] * lk1_ref[...], axis=-1, keepdims=True))
               - jnp.exp(jnp.sum(lq2_ref[...] * lk2_ref[...], axis=-1, keepdims=True))
               + lambda_init)
        _diff_attention(lam, sg_ref, q_ref, k_ref, v_ref, o_ref, lambda_init, tq)


def _mix(u, qkv, w_pool, pool_scale, lq1, lk1, lq2, lk2, subln_g, c_pad, w_ada, b_ada, ada_col0,
         n_heads, head_dim, seq, lambda_init):
    m, d_pool = u.shape
    n_groups = len(POOL_WINDOWS)
    hw = 2 * head_dim
    assert d_pool == n_groups * hw and w_pool.shape == (n_groups, hw, hw)
    n_b, n_hh = m // seq, n_groups + n_heads
    rows, d = c_pad.shape
    n_late = w_ada.shape[1] - ada_col0
    ta = n_late // (n_b * n_hh)
    assert ta * n_b * n_hh == n_late and ta % 128 == 0 and ada_col0 % ta == 0
    vec = lambda n: pl.BlockSpec((1, n), lambda b, hh: (0, 0))
    head = lambda hh: jnp.maximum(hh - n_groups, 0)
    group = lambda hh: jnp.minimum(hh, n_groups - 1)
    ada_block = lambda b, hh: (0, ada_col0 // ta + b * n_hh + hh)
    kern = functools.partial(_mix_kernel, lambda_init=lambda_init, tq=min(128, seq))
    return pl.pallas_call(
        kern,
        grid=(n_b, n_hh),
        in_specs=[
            vec(head_dim), vec(head_dim), vec(head_dim), vec(head_dim), vec(hw),
            pl.BlockSpec((seq, hw), lambda b, hh: (b, group(hh))),
            pl.BlockSpec(w_pool.shape, lambda b, hh: (0, 0, 0)),
            pl.BlockSpec((1, hw), lambda b, hh: (0, group(hh))),
            pl.BlockSpec((seq, hw), lambda b, hh: (b, head(hh))),
            pl.BlockSpec((seq, hw), lambda b, hh: (b, n_heads + head(hh))),
            pl.BlockSpec((seq, hw), lambda b, hh: (b, 2 * n_heads + head(hh))),
            pl.BlockSpec((rows, d), lambda b, hh: (0, 0)),
            pl.BlockSpec((d, ta), ada_block),
            pl.BlockSpec((1, ta), ada_block),
        ],
        out_specs=[
            pl.BlockSpec((seq, hw), lambda b, hh: (b, hh)),
            pl.BlockSpec((rows, ta), lambda b, hh: (0, b * n_hh + hh)),
        ],
        out_shape=[
            jax.ShapeDtypeStruct((m, d_pool + n_heads * hw), BF16),
            jax.ShapeDtypeStruct((rows, n_late), F32),
        ],
        compiler_params=_compiler_params(2),
        name="mix",
    )(lq1.reshape(1, -1), lk1.reshape(1, -1), lq2.reshape(1, -1), lk2.reshape(1, -1),
      subln_g.reshape(1, -1), u, w_pool, pool_scale.reshape(1, d_pool), qkv, qkv, qkv,
      c_pad, w_ada, b_ada)


def kernel(x, c, w_ada, b_ada, pre_norm_g, post_norm_g, ffn1_w1, ffn1_w3, ffn1_w2, w_in, w_pool,
           pool_scale, lambda_q1, lambda_k1, lambda_q2, lambda_k2, subln_g, w_out, ffn2_w1,
           ffn2_w3, ffn2_w2):
    b, s, d = x.shape
    depth = w_ada.shape[0]
    d_pool = pool_scale.shape[1]
    head_dim = lambda_q1.shape[1]
    d_in = w_in.shape[2]
    d_qk = (d_in - d_pool) // 3
    n_heads = d_qk // (2 * head_dim)
    dff = ffn1_w1.shape[2]
    ffn_tk = 256 if dff % 256 == 0 else dff
    xr = x.reshape(b * s, d)
    h = None
    for l in range(depth):
        lambda_init = 0.8 - 0.6 * math.exp(-0.3 * l)
        c_pad = jnp.pad(c, ((0, 8 - b), (0, 0)))
        b_ada_row = b_ada[l].reshape(1, -1)
        n_early = N_EARLY_MOD * d
        mod_early = _ada(c_pad, w_ada[l], b_ada_row, n_early)
        early = lambda i: mod_early[:b, i * d:(i + 1) * d].reshape(b, 1, d)
        pre_g = pre_norm_g[l].reshape(N_SUB, 1, d)
        post_g = post_norm_g[l].reshape(N_SUB, 1, d)

        if h is None:
            h = _prenorm(xr, early(0), early(1), pre_g[0], s)
        act = _ffn_up(h, ffn1_w1[l], ffn1_w3[l], s)
        xr, h = _down(act, ffn1_w2[l], xr, early(2), post_g[0], (early(3), early(4), pre_g[1]),
                      coef=0.5, tk=ffn_tk, seq=s)

        u, qkv = _proj(h, w_in[l], d_pool, d_qk, head_dim, s)
        mixed, mod_late = _mix(u, qkv, w_pool[l], pool_scale[l], lambda_q1[l], lambda_k1[l],
                               lambda_q2[l], lambda_k2[l], subln_g[l], c_pad, w_ada[l], b_ada_row,
                               n_early, n_heads, head_dim, s, lambda_init)
        late = lambda i: mod_late[:b, (i - N_EARLY_MOD) * d:(i + 1 - N_EARLY_MOD) * d].reshape(b, 1, d)
        xr, h = _down(mixed, w_out[l], xr, late(5), post_g[1], (late(6), late(7), pre_g[2]),
                      coef=1.0, tk=min(256, d), seq=s)

        act = _ffn_up(h, ffn2_w1[l], ffn2_w3[l], s)
        xr, h = _down(act, ffn2_w2[l], xr, late(8), post_g[2], None,
                      coef=0.5, tk=ffn_tk, seq=s)
    return xr.reshape(b, s, d)
```

```python
import functools
import math

import jax
import jax.numpy as jnp
from jax import lax
from jax.experimental import pallas as pl
from jax.experimental.pallas import tpu as pltpu

F32 = jnp.float32
BF16 = jnp.bfloat16

NORM_EPS = 1e-6
POOL_WINDOWS = (2, 4, 8, 16)
N_SUB = 3
N_EARLY_MOD = 5
V7X_VMEM_LIMIT_BYTES = 58 * 1024 * 1024
F32_SUBLANES = 8
NORM_ROWS = 8
ROW_TILE = 2048
PRENORM_ROWS = 512
EPILOGUE_ROWS = 128
UP_ROW_CHUNK = 512
X_SLOTS = 3
ADA_COLS = 512
FFN_COLS = 256
PROJ_COLS = 512
ATTN_Q_ROWS = 128


def _compiler_params(n_grid_dims):
    return pltpu.CompilerParams(
        dimension_semantics=("arbitrary",) * n_grid_dims,
        vmem_limit_bytes=V7X_VMEM_LIMIT_BYTES,
    )


def _inv_rms(xf):
    return lax.rsqrt(jnp.mean(xf * xf, axis=-1, keepdims=True) + NORM_EPS)


def _bdot(a, b):
    return jnp.dot(a, b, preferred_element_type=F32)


def _ada_block(c_ref, w_ref, b_ref, o_ref):
    c = c_ref[...]
    act = (c * jax.nn.sigmoid(c)).astype(BF16)
    o_ref[...] = _bdot(act, w_ref[...].astype(BF16)) + b_ref[...]


def _ada(c_pad, w_ada, b_ada, n_cols):
    rows, d = c_pad.shape
    tn = ADA_COLS if n_cols % ADA_COLS == 0 else n_cols
    return pl.pallas_call(
        _ada_block,
        grid=(n_cols // tn,),
        in_specs=[
            pl.BlockSpec((rows, d), lambda j: (0, 0)),
            pl.BlockSpec((d, tn), lambda j: (0, j)),
            pl.BlockSpec((1, tn), lambda j: (0, j)),
        ],
        out_specs=pl.BlockSpec((rows, tn), lambda j: (0, j)),
        out_shape=jax.ShapeDtypeStruct((rows, n_cols), F32),
        compiler_params=_compiler_params(1),
        name="ada",
    )(c_pad, w_ada, b_ada)


def _modulated_norm(xf, gs, shift):
    return ((xf * _inv_rms(xf)) * gs + shift).astype(BF16)


def _prenorm_kernel(x_ref, shift_ref, scale_ref, g_ref, h_ref):
    gs = g_ref[...] * (1.0 + scale_ref[0])
    shift = shift_ref[0]
    for r in range(0, x_ref.shape[0], NORM_ROWS):
        rows = slice(r, r + NORM_ROWS)
        h_ref[rows, :] = _modulated_norm(x_ref[rows, :], gs, shift)


def _prenorm(x, shift, scale, g, seq):
    m, d = x.shape
    tm = min(PRENORM_ROWS, seq)
    return pl.pallas_call(
        _prenorm_kernel,
        grid=(m // tm,),
        in_specs=[
            pl.BlockSpec((tm, d), lambda i: (i, 0)),
            pl.BlockSpec((1, 1, d), lambda i: (i * tm // seq, 0, 0)),
            pl.BlockSpec((1, 1, d), lambda i: (i * tm // seq, 0, 0)),
            pl.BlockSpec((1, d), lambda i: (0, 0)),
        ],
        out_specs=pl.BlockSpec((tm, d), lambda i: (i, 0)),
        out_shape=jax.ShapeDtypeStruct((m, d), BF16),
        compiler_params=_compiler_params(1),
        name="prenorm",
    )(x, shift, scale, g)


def _ffn_up_kernel(h_ref, w1_ref, w3_ref, act_ref):
    w1 = w1_ref[...].astype(BF16)
    w3 = w3_ref[...].astype(BF16)
    chunk = min(UP_ROW_CHUNK, h_ref.shape[0])
    for r in range(0, h_ref.shape[0], chunk):
        h = h_ref[r:r + chunk, :]
        a = _bdot(h, w1)
        b = _bdot(h, w3)
        act_ref[r:r + chunk, :] = (a * jax.nn.sigmoid(a) * b).astype(BF16)


def _ffn_up(h, w1, w3, seq):
    m, d = h.shape
    dff = w1.shape[1]
    tm = min(ROW_TILE, seq)
    tn = FFN_COLS if dff % FFN_COLS == 0 else dff
    return pl.pallas_call(
        _ffn_up_kernel,
        grid=(m // tm, dff // tn),
        in_specs=[
            pl.BlockSpec((tm, d), lambda i, j: (i, 0)),
            pl.BlockSpec((d, tn), lambda i, j: (0, j)),
            pl.BlockSpec((d, tn), lambda i, j: (0, j)),
        ],
        out_specs=pl.BlockSpec((tm, tn), lambda i, j: (i, j)),
        out_shape=jax.ShapeDtypeStruct((m, dff), BF16),
        compiler_params=_compiler_params(2),
        name="ffn_up",
    )(h, w1, w3)


def _down_kernel(act_ref, w_ref, gate_ref, post_g_ref, *rest, n_k, coef, emit_h):
    if emit_h:
        (shift_ref, scale_ref, g_ref, x_hbm, o_hbm, h_hbm,
         acc_ref, xbuf, obuf, hbuf, sem_x, sem_o, sem_h) = rest
    else:
        x_hbm, o_hbm, acc_ref, xbuf, obuf, sem_x, sem_o = rest
    k = pl.program_id(1)
    tm = acc_ref.shape[0]
    te = xbuf.shape[1]
    n_chunks = tm // te
    row0 = pl.program_id(0) * tm

    def chunk_rows(c):
        return pl.ds(pl.multiple_of(row0 + c * te, te), te)

    def x_copy(c, slot):
        return pltpu.make_async_copy(x_hbm.at[chunk_rows(c), :], xbuf.at[slot], sem_x.at[slot])

    def o_copy(c, slot):
        return pltpu.make_async_copy(obuf.at[slot], o_hbm.at[chunk_rows(c), :], sem_o.at[slot])

    def h_copy(c, slot):
        return pltpu.make_async_copy(hbuf.at[slot], h_hbm.at[chunk_rows(c), :], sem_h.at[slot])

    def wait_results(c, slot):
        o_copy(c, slot).wait()
        if emit_h:
            h_copy(c, slot).wait()

    @pl.when(k == n_k - 1)
    def _():
        for c in range(min(X_SLOTS, n_chunks)):
            x_copy(c, c).start()

    @pl.when(k == 0)
    def _():
        acc_ref[...] = _bdot(act_ref[...], w_ref[...].astype(BF16))

    @pl.when((k > 0) & (k < n_k))
    def _():
        acc_ref[...] += _bdot(act_ref[...], w_ref[...].astype(BF16))

    @pl.when(k == n_k)
    def _():
        cg = coef * gate_ref[0]
        pg = post_g_ref[...]
        if emit_h:
            gs = g_ref[...] * (1.0 + scale_ref[0])
            shift = shift_ref[0]

        def chunk(c, carry):
            xs = c % X_SLOTS
            rs = c % 2
            x_copy(c, xs).wait()

            @pl.when(c >= 2)
            def _():
                wait_results(c - 2, rs)

            base = c * te
            for r in range(0, te, NORM_ROWS):
                rows = slice(r, r + NORM_ROWS)
                f = acc_ref[pl.ds(pl.multiple_of(base + r, NORM_ROWS), NORM_ROWS), :]
                out = xbuf[xs, rows, :] + cg * ((f * _inv_rms(f)) * pg)
                obuf[rs, rows, :] = out
                if emit_h:
                    hbuf[rs, rows, :] = _modulated_norm(out, gs, shift)
            o_copy(c, rs).start()
            if emit_h:
                h_copy(c, rs).start()

            @pl.when(c + X_SLOTS < n_chunks)
            def _():
                x_copy(c + X_SLOTS, xs).start()

            return carry

        lax.fori_loop(0, n_chunks, chunk, 0)
        for c in range(max(n_chunks - 2, 0), n_chunks):
            wait_results(c, c % 2)


def _down(act, w, x, gate, post_g, next_mod, *, coef, tk, seq):
    m, d = x.shape
    kdim = act.shape[1]
    tm = min(ROW_TILE, seq)
    te = min(EPILOGUE_ROWS, tm)
    assert kdim % tk == 0 and tm % te == 0
    n_k = kdim // tk
    emit_h = next_mod is not None
    k_tile = lambda k: jnp.minimum(k, n_k - 1)
    batch_vec = pl.BlockSpec((1, 1, d), lambda i, k: (i * tm // seq, 0, 0))
    row_vec = pl.BlockSpec((1, d), lambda i, k: (0, 0))
    hbm = pl.BlockSpec(memory_space=pl.ANY)
    in_specs = [
        pl.BlockSpec((tm, tk), lambda i, k: (i, k_tile(k))),
        pl.BlockSpec((tk, d), lambda i, k: (k_tile(k), 0)),
        batch_vec, row_vec,
    ]
    args = [act, w, gate, post_g]
    if emit_h:
        in_specs += [batch_vec, batch_vec, row_vec]
        args += list(next_mod)
    in_specs.append(hbm)
    args.append(x)
    out_specs = [hbm]
    out_shape = [jax.ShapeDtypeStruct((m, d), F32)]
    scratch = [pltpu.VMEM((tm, d), F32), pltpu.VMEM((X_SLOTS, te, d), F32),
               pltpu.VMEM((2, te, d), F32)]
    sems = [pltpu.SemaphoreType.DMA((X_SLOTS,)), pltpu.SemaphoreType.DMA((2,))]
    if emit_h:
        out_specs.append(hbm)
        out_shape.append(jax.ShapeDtypeStruct((m, d), BF16))
        scratch.append(pltpu.VMEM((2, te, d), BF16))
        sems.append(pltpu.SemaphoreType.DMA((2,)))
    kern = functools.partial(_down_kernel, n_k=n_k, coef=coef, emit_h=emit_h)
    res = pl.pallas_call(
        kern,
        grid=(m // tm, n_k + 1),
        in_specs=in_specs,
        out_specs=out_specs,
        out_shape=out_shape,
        scratch_shapes=scratch + sems,
        compiler_params=_compiler_params(2),
        name="down",
    )(*args)
    return (res[0], res[1]) if emit_h else (res[0], None)


def _proj_kernel(h_ref, w_ref, u_ref, qkv_ref, *, n_u_tiles, n_q_tiles, q_scale):
    j = pl.program_id(1)

    @pl.when(j < n_u_tiles)
    def _():
        u_ref[...] = _bdot(h_ref[...], w_ref[...].astype(BF16))

    @pl.when(j >= n_u_tiles)
    def _():
        factor = jnp.where(j < n_u_tiles + n_q_tiles, q_scale, 1.0).astype(F32)
        qkv_ref[...] = (_bdot(h_ref[...], w_ref[...].astype(BF16)) * factor).astype(BF16)


def _proj(h, w_in, d_pool, d_qk, head_dim, seq):
    m, d = h.shape
    d_in = w_in.shape[1]
    tn = min(PROJ_COLS, d_pool)
    assert d_pool % tn == 0 and d_qk % tn == 0
    n_u = d_pool // tn
    tm = min(ROW_TILE, seq)
    kern = functools.partial(_proj_kernel, n_u_tiles=n_u, n_q_tiles=d_qk // tn,
                             q_scale=math.log2(math.e) * head_dim ** -0.5)
    return pl.pallas_call(
        kern,
        grid=(m // tm, d_in // tn),
        in_specs=[
            pl.BlockSpec((tm, d), lambda i, j: (i, 0), pipeline_mode=pl.Buffered(1)),
            pl.BlockSpec((d, tn), lambda i, j: (0, j)),
        ],
        out_specs=[
            pl.BlockSpec((tm, tn), lambda i, j: (i, jnp.minimum(j, n_u - 1))),
            pl.BlockSpec((tm, tn), lambda i, j: (i, jnp.maximum(j - n_u, 0))),
        ],
        out_shape=[
            jax.ShapeDtypeStruct((m, d_pool), F32),
            jax.ShapeDtypeStruct((m, d_in - d_pool), BF16),
        ],
        compiler_params=_compiler_params(2),
        name="proj",
    )(h, w_in)


def _pool_group(u_ref, w_ref, ps_ref, o_ref, group):
    win = POOL_WINDOWS[group]
    ug = u_ref[...]
    t = lax.broadcasted_iota(jnp.int32, ug.shape, 0)
    s = ug
    k = 1
    while k < win:
        s = s + jnp.where(t >= k, pltpu.roll(s, k, axis=0), 0.0)
        k *= 2
    count = jnp.minimum(t + 1, win).astype(F32)
    y = (s / count - ug).astype(BF16)
    o_ref[...] = (_bdot(y, w_ref[group].astype(BF16)) * ps_ref[...]).astype(BF16)


def _diff_attention(lam, sg_ref, q_ref, k_ref, v_ref, o_ref, lambda_init, tq):
    seq = q_ref.shape[0]
    hd = q_ref.shape[1] // 2
    sg = sg_ref[...]
    nt_dims = (((1,), (1,)), ((), ()))
    on_or_below_diag = (lax.broadcasted_iota(jnp.int32, (tq, tq), 1)
                        <= lax.broadcasted_iota(jnp.int32, (tq, tq), 0))
    n_blocks = seq // tq

    def scores(i):
        kv = (i + 1) * tq
        return [lax.dot_general(q_ref[i * tq:kv, cols], k_ref[0:kv, cols], nt_dims,
                                preferred_element_type=F32)
                for cols in (slice(0, hd), slice(hd, 2 * hd))]

    def exp_scores(i, s):
        kv = (i + 1) * tq
        diag = jnp.where(on_or_below_diag, s[:, kv - tq:], -jnp.inf)
        s = diag if i == 0 else jnp.concatenate([s[:, :kv - tq], diag], axis=1)
        p = jnp.exp2(s - jnp.max(s, axis=-1, keepdims=True))
        return p, jnp.sum(p, axis=-1, keepdims=True)

    def weights(i, s):
        p1, l1 = exp_scores(i, s[0])
        p2, l2 = exp_scores(i, s[1])
        return (p1 - p2 * (lam * l1 / l2)).astype(BF16), 1.0 / l1

    def finish(i, a, inv_l1):
        kv = (i + 1) * tq
        o = _bdot(a, v_ref[0:kv, :]) * inv_l1
        o_ref[i * tq:kv, :] = (((o * _inv_rms(o)) * sg) * (1.0 - lambda_init)).astype(BF16)

    s_next = scores(0)
    pending = None
    for i in range(n_blocks):
        s_cur = s_next
        if i + 1 < n_blocks:
            s_next = scores(i + 1)
        cur = (i,) + weights(i, s_cur)
        if pending is not None:
            finish(*pending)
        pending = cur
    finish(*pending)


def _mix_kernel(lq1_ref, lk1_ref, lq2_ref, lk2_ref, sg_ref, u_ref, wp_ref, ps_ref,
                q_ref, k_ref, v_ref, c_ref, wa_ref, ba_ref, o_ref, mod_ref, *, lambda_init, tq):
    n_groups = len(POOL_WINDOWS)
    hh = pl.program_id(1)

    for g in range(n_groups):
        @pl.when(hh == g)
        def _(g=g):
            _ada_block(c_ref, wa_ref, ba_ref, mod_ref)
            _pool_group(u_ref, wp_ref, ps_ref, o_ref, g)

    @pl.when(hh >= n_groups)
    def _():
        _ada_block(c_ref, wa_ref, ba_ref, mod_ref)
        lam = (jnp.exp(jnp.sum(lq1_ref[...] * lk1_ref[...], axis=-1, keepdims=True))
               - jnp.exp(jnp.sum(lq2_ref[...] * lk2_ref[...], axis=-1, keepdims=True))
               + lambda_init)
        _diff_attention(lam, sg_ref, q_ref, k_ref, v_ref, o_ref, lambda_init, tq)


def _mix(u, qkv, w_pool, pool_scale, lq1, lk1, lq2, lk2, subln_g, c_pad, w_ada, b_ada, ada_col0,
         n_heads, head_dim, seq, lambda_init):
    m, d_pool = u.shape
    n_groups = len(POOL_WINDOWS)
    hw = 2 * head_dim
    assert d_pool == n_groups * hw and w_pool.shape == (n_groups, hw, hw)
    n_b, n_hh = m // seq, n_groups + n_heads
    rows, d = c_pad.shape
    n_late = w_ada.shape[1] - ada_col0
    ta = n_late // (n_b * n_hh)
    assert ta * n_b * n_hh == n_late and ta % 128 == 0 and ada_col0 % ta == 0
    vec = lambda n: pl.BlockSpec((1, n), lambda b, hh: (0, 0))
    head = lambda hh: jnp.maximum(hh - n_groups, 0)
    group = lambda hh: jnp.minimum(hh, n_groups - 1)
    ada_block = lambda b, hh: (0, ada_col0 // ta + b * n_hh + hh)
    kern = functools.partial(_mix_kernel, lambda_init=lambda_init, tq=min(ATTN_Q_ROWS, seq))
    return pl.pallas_call(
        kern,
        grid=(n_b, n_hh),
        in_specs=[
            vec(head_dim), vec(head_dim), vec(head_dim), vec(head_dim), vec(hw),
            pl.BlockSpec((seq, hw), lambda b, hh: (b, group(hh))),
            pl.BlockSpec(w_pool.shape, lambda b, hh: (0, 0, 0)),
            pl.BlockSpec((1, hw), lambda b, hh: (0, group(hh))),
            pl.BlockSpec((seq, hw), lambda b, hh: (b, head(hh))),
            pl.BlockSpec((seq, hw), lambda b, hh: (b, n_heads + head(hh))),
            pl.BlockSpec((seq, hw), lambda b, hh: (b, 2 * n_heads + head(hh))),
            pl.BlockSpec((rows, d), lambda b, hh: (0, 0)),
            pl.BlockSpec((d, ta), ada_block),
            pl.BlockSpec((1, ta), ada_block),
        ],
        out_specs=[
            pl.BlockSpec((seq, hw), lambda b, hh: (b, hh)),
            pl.BlockSpec((rows, ta), lambda b, hh: (0, b * n_hh + hh)),
        ],
        out_shape=[
            jax.ShapeDtypeStruct((m, d_pool + n_heads * hw), BF16),
            jax.ShapeDtypeStruct((rows, n_late), F32),
        ],
        compiler_params=_compiler_params(2),
        name="mix",
    )(lq1.reshape(1, -1), lk1.reshape(1, -1), lq2.reshape(1, -1), lk2.reshape(1, -1),
      subln_g.reshape(1, -1), u, w_pool, pool_scale.reshape(1, d_pool), qkv, qkv, qkv,
      c_pad, w_ada, b_ada)


def kernel(x, c, w_ada, b_ada, pre_norm_g, post_norm_g, ffn1_w1, ffn1_w3, ffn1_w2, w_in, w_pool,
           pool_scale, lambda_q1, lambda_k1, lambda_q2, lambda_k2, subln_g, w_out, ffn2_w1,
           ffn2_w3, ffn2_w2):
    b, s, d = x.shape
    depth = w_ada.shape[0]
    d_pool = pool_scale.shape[1]
    head_dim = lambda_q1.shape[1]
    d_in = w_in.shape[2]
    d_qk = (d_in - d_pool) // 3
    n_heads = d_qk // (2 * head_dim)
    dff = ffn1_w1.shape[2]
    ffn_tk = FFN_COLS if dff % FFN_COLS == 0 else dff
    xr = x.reshape(b * s, d)
    h = None
    for l in range(depth):
        lambda_init = 0.8 - 0.6 * math.exp(-0.3 * l)
        c_pad = jnp.pad(c, ((0, -b % F32_SUBLANES), (0, 0)))
        b_ada_row = b_ada[l].reshape(1, -1)
        n_early = N_EARLY_MOD * d
        mod_early = _ada(c_pad, w_ada[l], b_ada_row, n_early)
        early = lambda i: mod_early[:b, i * d:(i + 1) * d].reshape(b, 1, d)
        pre_g = pre_norm_g[l].reshape(N_SUB, 1, d)
        post_g = post_norm_g[l].reshape(N_SUB, 1, d)

        if h is None:
            h = _prenorm(xr, early(0), early(1), pre_g[0], s)
        act = _ffn_up(h, ffn1_w1[l], ffn1_w3[l], s)
        xr, h = _down(act, ffn1_w2[l], xr, early(2), post_g[0], (early(3), early(4), pre_g[1]),
                      coef=0.5, tk=ffn_tk, seq=s)

        u, qkv = _proj(h, w_in[l], d_pool, d_qk, head_dim, s)
        mixed, mod_late = _mix(u, qkv, w_pool[l], pool_scale[l], lambda_q1[l], lambda_k1[l],
                               lambda_q2[l], lambda_k2[l], subln_g[l], c_pad, w_ada[l], b_ada_row,
                               n_early, n_heads, head_dim, s, lambda_init)
        late = lambda i: mod_late[:b, (i - N_EARLY_MOD) * d:(i + 1 - N_EARLY_MOD) * d].reshape(b, 1, d)
        xr, h = _down(mixed, w_out[l], xr, late(5), post_g[1], (late(6), late(7), pre_g[2]),
                      coef=1.0, tk=min(FFN_COLS, d), seq=s)

        act = _ffn_up(h, ffn2_w1[l], ffn2_w3[l], s)
        xr, h = _down(act, ffn2_w2[l], xr, late(8), post_g[2], None,
                      coef=0.5, tk=ffn_tk, seq=s)
    return xr.reshape(b, s, d)
```

```python
import functools
import math

import jax
import jax.numpy as jnp
from jax import lax
from jax.experimental import pallas as pl
from jax.experimental.pallas import tpu as pltpu

F32 = jnp.float32
BF16 = jnp.bfloat16

NORM_EPS = 1e-6
POOL_WINDOWS = (2, 4, 8, 16)
N_SUB = 3
N_EARLY_MOD = 5
V7X_VMEM_LIMIT_BYTES = 58 * 1024 * 1024
F32_SUBLANES = 8
NORM_ROWS = 8
ROW_TILE = 2048
PRENORM_ROWS = 512
EPILOGUE_ROWS = 128
UP_ROW_CHUNK = 512
X_SLOTS = 3
ADA_COLS = 512
FFN_COLS = 256
PROJ_COLS = 512
ATTN_Q_ROWS = 128


def _compiler_params(n_grid_dims):
    return pltpu.CompilerParams(
        dimension_semantics=("arbitrary",) * n_grid_dims,
        vmem_limit_bytes=V7X_VMEM_LIMIT_BYTES,
    )


def _inv_rms(xf):
    return lax.rsqrt(jnp.mean(xf * xf, axis=-1, keepdims=True) + NORM_EPS)


def _bdot(a, b):
    return jnp.dot(a, b, preferred_element_type=F32)


def _ada_block(c_ref, w_ref, b_ref, o_ref):
    c = c_ref[...]
    act = (c * jax.nn.sigmoid(c)).astype(BF16)
    o_ref[...] = _bdot(act, w_ref[...].astype(BF16)) + b_ref[...]


def _ada(c_pad, w_ada, b_ada, n_cols):
    rows, d = c_pad.shape
    tn = ADA_COLS if n_cols % ADA_COLS == 0 else n_cols
    return pl.pallas_call(
        _ada_block,
        grid=(n_cols // tn,),
        in_specs=[
            pl.BlockSpec((rows, d), lambda j: (0, 0)),
            pl.BlockSpec((d, tn), lambda j: (0, j)),
            pl.BlockSpec((1, tn), lambda j: (0, j)),
        ],
        out_specs=pl.BlockSpec((rows, tn), lambda j: (0, j)),
        out_shape=jax.ShapeDtypeStruct((rows, n_cols), F32),
        compiler_params=_compiler_params(1),
        name="ada",
    )(c_pad, w_ada, b_ada)


def _modulated_norm(xf, gs, shift):
    return ((xf * _inv_rms(xf)) * gs + shift).astype(BF16)


def _prenorm_kernel(x_ref, shift_ref, scale_ref, g_ref, h_ref):
    gs = g_ref[...] * (1.0 + scale_ref[0])
    shift = shift_ref[0]
    for r in range(0, x_ref.shape[0], NORM_ROWS):
        rows = slice(r, r + NORM_ROWS)
        h_ref[rows, :] = _modulated_norm(x_ref[rows, :], gs, shift)


def _prenorm(x, shift, scale, g, seq):
    m, d = x.shape
    tm = min(PRENORM_ROWS, seq)
    return pl.pallas_call(
        _prenorm_kernel,
        grid=(m // tm,),
        in_specs=[
            pl.BlockSpec((tm, d), lambda i: (i, 0)),
            pl.BlockSpec((1, 1, d), lambda i: (i * tm // seq, 0, 0)),
            pl.BlockSpec((1, 1, d), lambda i: (i * tm // seq, 0, 0)),
            pl.BlockSpec((1, d), lambda i: (0, 0)),
        ],
        out_specs=pl.BlockSpec((tm, d), lambda i: (i, 0)),
        out_shape=jax.ShapeDtypeStruct((m, d), BF16),
        compiler_params=_compiler_params(1),
        name="prenorm",
    )(x, shift, scale, g)


def _ffn_up_kernel(h_ref, w1_ref, w3_ref, act_ref):
    w1 = w1_ref[...].astype(BF16)
    w3 = w3_ref[...].astype(BF16)
    chunk = min(UP_ROW_CHUNK, h_ref.shape[0])
    for r in range(0, h_ref.shape[0], chunk):
        h = h_ref[r:r + chunk, :]
        a = _bdot(h, w1)
        b = _bdot(h, w3)
        act_ref[r:r + chunk, :] = (a * jax.nn.sigmoid(a) * b).astype(BF16)


def _ffn_up(h, w1, w3, seq):
    m, d = h.shape
    dff = w1.shape[1]
    tm = min(ROW_TILE, seq)
    tn = FFN_COLS if dff % FFN_COLS == 0 else dff
    return pl.pallas_call(
        _ffn_up_kernel,
        grid=(m // tm, dff // tn),
        in_specs=[
            pl.BlockSpec((tm, d), lambda i, j: (i, 0)),
            pl.BlockSpec((d, tn), lambda i, j: (0, j)),
            pl.BlockSpec((d, tn), lambda i, j: (0, j)),
        ],
        out_specs=pl.BlockSpec((tm, tn), lambda i, j: (i, j)),
        out_shape=jax.ShapeDtypeStruct((m, dff), BF16),
        compiler_params=_compiler_params(2),
        name="ffn_up",
    )(h, w1, w3)


def _down_kernel(act_ref, w_ref, gate_ref, post_g_ref, *rest, n_k, coef, emit_h):
    if emit_h:
        (shift_ref, scale_ref, g_ref, x_hbm, o_hbm, h_hbm,
         acc_ref, xbuf, obuf, hbuf, sem_x, sem_o, sem_h) = rest
    else:
        x_hbm, o_hbm, acc_ref, xbuf, obuf, sem_x, sem_o = rest
    k = pl.program_id(1)
    tm = acc_ref.shape[0]
    te = xbuf.shape[1]
    n_chunks = tm // te
    tile = pl.program_id(0)
    row0 = tile * tm

    def chunk_rows(c):
        return pl.ds(pl.multiple_of(row0 + c * te, te), te)

    def x_copy(c, slot):
        return pltpu.make_async_copy(x_hbm.at[chunk_rows(c), :], xbuf.at[slot], sem_x.at[slot])

    def o_copy(c, slot):
        return pltpu.make_async_copy(obuf.at[slot], o_hbm.at[chunk_rows(c), :], sem_o.at[slot])

    def h_copy(c, slot):
        return pltpu.make_async_copy(hbuf.at[slot], h_hbm.at[chunk_rows(c), :], sem_h.at[slot])

    def wait_results(c, slot):
        o_copy(c, slot).wait()
        if emit_h:
            h_copy(c, slot).wait()

    @pl.when(k == n_k - 1)
    def _():
        for c in range(min(X_SLOTS, n_chunks)):
            x_copy(c, c).start()

    @pl.when(k == 0)
    def _():
        acc_ref[...] = _bdot(act_ref[...], w_ref[...].astype(BF16))

    @pl.when((k > 0) & (k < n_k))
    def _():
        acc_ref[...] += _bdot(act_ref[...], w_ref[...].astype(BF16))

    @pl.when(k == n_k)
    def _():
        cg = coef * gate_ref[0]
        pg = post_g_ref[...]
        if emit_h:
            gs = g_ref[...] * (1.0 + scale_ref[0])
            shift = shift_ref[0]

        def chunk(c, carry):
            xs = c % X_SLOTS
            rs = c % 2
            x_copy(c, xs).wait()

            @pl.when((c >= 2) | (tile > 0))
            def _():
                wait_results(c - 2, rs)

            base = c * te
            for r in range(0, te, NORM_ROWS):
                rows = slice(r, r + NORM_ROWS)
                f = acc_ref[pl.ds(pl.multiple_of(base + r, NORM_ROWS), NORM_ROWS), :]
                out = xbuf[xs, rows, :] + cg * ((f * _inv_rms(f)) * pg)
                obuf[rs, rows, :] = out
                if emit_h:
                    hbuf[rs, rows, :] = _modulated_norm(out, gs, shift)
            o_copy(c, rs).start()
            if emit_h:
                h_copy(c, rs).start()

            @pl.when(c + X_SLOTS < n_chunks)
            def _():
                x_copy(c + X_SLOTS, xs).start()

            return carry

        lax.fori_loop(0, n_chunks, chunk, 0)

        @pl.when(tile == pl.num_programs(0) - 1)
        def _():
            for c in range(n_chunks - 2, n_chunks):
                wait_results(c, c % 2)


def _down(act, w, x, gate, post_g, next_mod, *, coef, tk, seq):
    m, d = x.shape
    kdim = act.shape[1]
    tm = min(ROW_TILE, seq)
    te = min(EPILOGUE_ROWS, tm)
    assert kdim % tk == 0 and tm % (2 * te) == 0
    n_k = kdim // tk
    emit_h = next_mod is not None
    k_tile = lambda k: jnp.minimum(k, n_k - 1)
    batch_vec = pl.BlockSpec((1, 1, d), lambda i, k: (i * tm // seq, 0, 0))
    row_vec = pl.BlockSpec((1, d), lambda i, k: (0, 0))
    hbm = pl.BlockSpec(memory_space=pl.ANY)
    in_specs = [
        pl.BlockSpec((tm, tk), lambda i, k: (i, k_tile(k))),
        pl.BlockSpec((tk, d), lambda i, k: (k_tile(k), 0)),
        batch_vec, row_vec,
    ]
    args = [act, w, gate, post_g]
    if emit_h:
        in_specs += [batch_vec, batch_vec, row_vec]
        args += list(next_mod)
    in_specs.append(hbm)
    args.append(x)
    out_specs = [hbm]
    out_shape = [jax.ShapeDtypeStruct((m, d), F32)]
    scratch = [pltpu.VMEM((tm, d), F32), pltpu.VMEM((X_SLOTS, te, d), F32),
               pltpu.VMEM((2, te, d), F32)]
    sems = [pltpu.SemaphoreType.DMA((X_SLOTS,)), pltpu.SemaphoreType.DMA((2,))]
    if emit_h:
        out_specs.append(hbm)
        out_shape.append(jax.ShapeDtypeStruct((m, d), BF16))
        scratch.append(pltpu.VMEM((2, te, d), BF16))
        sems.append(pltpu.SemaphoreType.DMA((2,)))
    kern = functools.partial(_down_kernel, n_k=n_k, coef=coef, emit_h=emit_h)
    res = pl.pallas_call(
        kern,
        grid=(m // tm, n_k + 1),
        in_specs=in_specs,
        out_specs=out_specs,
        out_shape=out_shape,
        scratch_shapes=scratch + sems,
        compiler_params=_compiler_params(2),
        name="down",
    )(*args)
    return (res[0], res[1]) if emit_h else (res[0], None)


def _proj_kernel(h_ref, w_ref, u_ref, qkv_ref, *, n_u_tiles, n_q_tiles, q_scale):
    j = pl.program_id(1)

    @pl.when(j < n_u_tiles)
    def _():
        u_ref[...] = _bdot(h_ref[...], w_ref[...].astype(BF16))

    @pl.when(j >= n_u_tiles)
    def _():
        factor = jnp.where(j < n_u_tiles + n_q_tiles, q_scale, 1.0).astype(F32)
        qkv_ref[...] = (_bdot(h_ref[...], w_ref[...].astype(BF16)) * factor).astype(BF16)


def _proj(h, w_in, d_pool, d_qk, head_dim, seq):
    m, d = h.shape
    d_in = w_in.shape[1]
    tn = min(PROJ_COLS, d_pool)
    assert d_pool % tn == 0 and d_qk % tn == 0
    n_u = d_pool // tn
    tm = min(ROW_TILE, seq)
    kern = functools.partial(_proj_kernel, n_u_tiles=n_u, n_q_tiles=d_qk // tn,
                             q_scale=math.log2(math.e) * head_dim ** -0.5)
    return pl.pallas_call(
        kern,
        grid=(m // tm, d_in // tn),
        in_specs=[
            pl.BlockSpec((tm, d), lambda i, j: (i, 0), pipeline_mode=pl.Buffered(1)),
            pl.BlockSpec((d, tn), lambda i, j: (0, j)),
        ],
        out_specs=[
            pl.BlockSpec((tm, tn), lambda i, j: (i, jnp.minimum(j, n_u - 1))),
            pl.BlockSpec((tm, tn), lambda i, j: (i, jnp.maximum(j - n_u, 0))),
        ],
        out_shape=[
            jax.ShapeDtypeStruct((m, d_pool), F32),
            jax.ShapeDtypeStruct((m, d_in - d_pool), BF16),
        ],
        compiler_params=_compiler_params(2),
        name="proj",
    )(h, w_in)


def _pool_group(u_ref, w_ref, ps_ref, o_ref, group):
    win = POOL_WINDOWS[group]
    ug = u_ref[...]
    t = lax.broadcasted_iota(jnp.int32, ug.shape, 0)
    s = ug
    k = 1
    while k < win:
        s = s + jnp.where(t >= k, pltpu.roll(s, k, axis=0), 0.0)
        k *= 2
    count = jnp.minimum(t + 1, win).astype(F32)
    y = (s / count - ug).astype(BF16)
    o_ref[...] = (_bdot(y, w_ref[group].astype(BF16)) * ps_ref[...]).astype(BF16)


def _diff_attention(lam, sg_ref, q_ref, k_ref, v_ref, o_ref, lambda_init, tq):
    seq = q_ref.shape[0]
    hd = q_ref.shape[1] // 2
    sg = sg_ref[...]
    nt_dims = (((1,), (1,)), ((), ()))
    on_or_below_diag = (lax.broadcasted_iota(jnp.int32, (tq, tq), 1)
                        <= lax.broadcasted_iota(jnp.int32, (tq, tq), 0))
    n_blocks = seq // tq

    def scores(i):
        kv = (i + 1) * tq
        return [lax.dot_general(q_ref[i * tq:kv, cols], k_ref[0:kv, cols], nt_dims,
                                preferred_element_type=F32)
                for cols in (slice(0, hd), slice(hd, 2 * hd))]

    def exp_scores(i, s):
        kv = (i + 1) * tq
        diag = jnp.where(on_or_below_diag, s[:, kv - tq:], -jnp.inf)
        s = diag if i == 0 else jnp.concatenate([s[:, :kv - tq], diag], axis=1)
        p = jnp.exp2(s - jnp.max(s, axis=-1, keepdims=True))
        return p, jnp.sum(p, axis=-1, keepdims=True)

    def weights(i, s):
        p1, l1 = exp_scores(i, s[0])
        p2, l2 = exp_scores(i, s[1])
        return (p1 - p2 * (lam * l1 / l2)).astype(BF16), 1.0 / l1

    def finish(i, a, inv_l1):
        kv = (i + 1) * tq
        o = _bdot(a, v_ref[0:kv, :]) * inv_l1
        o_ref[i * tq:kv, :] = (((o * _inv_rms(o)) * sg) * (1.0 - lambda_init)).astype(BF16)

    s_next = scores(0)
    pending = None
    for i in range(n_blocks):
        s_cur = s_next
        if i + 1 < n_blocks:
            s_next = scores(i + 1)
        cur = (i,) + weights(i, s_cur)
        if pending is not None:
            finish(*pending)
        pending = cur
    finish(*pending)


def _mix_kernel(lq1_ref, lk1_ref, lq2_ref, lk2_ref, sg_ref, u_ref, wp_ref, ps_ref,
                q_ref, k_ref, v_ref, c_ref, wa_ref, ba_ref, o_ref, mod_ref, *, lambda_init, tq):
    n_groups = len(POOL_WINDOWS)
    hh = pl.program_id(1)

    for g in range(n_groups):
        @pl.when(hh == g)
        def _(g=g):
            _ada_block(c_ref, wa_ref, ba_ref, mod_ref)
            _pool_group(u_ref, wp_ref, ps_ref, o_ref, g)

    @pl.when(hh >= n_groups)
    def _():
        _ada_block(c_ref, wa_ref, ba_ref, mod_ref)
        lam = (jnp.exp(jnp.sum(lq1_ref[...] * lk1_ref[...], axis=-1, keepdims=True))
               - jnp.exp(jnp.sum(lq2_ref[...] * lk2_ref[...], axis=-1, keepdims=True))
               + lambda_init)
        _diff_attention(lam, sg_ref, q_ref, k_ref, v_ref, o_ref, lambda_init, tq)


def _mix(u, qkv, w_pool, pool_scale, lq1, lk1, lq2, lk2, subln_g, c_pad, w_ada, b_ada, ada_col0,
         n_heads, head_dim, seq, lambda_init):
    m, d_pool = u.shape
    n_groups = len(POOL_WINDOWS)
    hw = 2 * head_dim
    assert d_pool == n_groups * hw and w_pool.shape == (n_groups, hw, hw)
    n_b, n_hh = m // seq, n_groups + n_heads
    rows, d = c_pad.shape
    n_late = w_ada.shape[1] - ada_col0
    ta = n_late // (n_b * n_hh)
    assert ta * n_b * n_hh == n_late and ta % 128 == 0 and ada_col0 % ta == 0
    vec = lambda n: pl.BlockSpec((1, n), lambda b, hh: (0, 0))
    head = lambda hh: jnp.maximum(hh - n_groups, 0)
    group = lambda hh: jnp.minimum(hh, n_groups - 1)
    ada_block = lambda b, hh: (0, ada_col0 // ta + b * n_hh + hh)
    kern = functools.partial(_mix_kernel, lambda_init=lambda_init, tq=min(ATTN_Q_ROWS, seq))
    return pl.pallas_call(
        kern,
        grid=(n_b, n_hh),
        in_specs=[
            vec(head_dim), vec(head_dim), vec(head_dim), vec(head_dim), vec(hw),
            pl.BlockSpec((seq, hw), lambda b, hh: (b, group(hh))),
            pl.BlockSpec(w_pool.shape, lambda b, hh: (0, 0, 0)),
            pl.BlockSpec((1, hw), lambda b, hh: (0, group(hh))),
            pl.BlockSpec((seq, hw), lambda b, hh: (b, head(hh))),
            pl.BlockSpec((seq, hw), lambda b, hh: (b, n_heads + head(hh))),
            pl.BlockSpec((seq, hw), lambda b, hh: (b, 2 * n_heads + head(hh))),
            pl.BlockSpec((rows, d), lambda b, hh: (0, 0)),
            pl.BlockSpec((d, ta), ada_block),
            pl.BlockSpec((1, ta), ada_block),
        ],
        out_specs=[
            pl.BlockSpec((seq, hw), lambda b, hh: (b, hh)),
            pl.BlockSpec((rows, ta), lambda b, hh: (0, b * n_hh + hh)),
        ],
        out_shape=[
            jax.ShapeDtypeStruct((m, d_pool + n_heads * hw), BF16),
            jax.ShapeDtypeStruct((rows, n_late), F32),
        ],
        compiler_params=_compiler_params(2),
        name="mix",
    )(lq1.reshape(1, -1), lk1.reshape(1, -1), lq2.reshape(1, -1), lk2.reshape(1, -1),
      subln_g.reshape(1, -1), u, w_pool, pool_scale.reshape(1, d_pool), qkv, qkv, qkv,
      c_pad, w_ada, b_ada)


def kernel(x, c, w_ada, b_ada, pre_norm_g, post_norm_g, ffn1_w1, ffn1_w3, ffn1_w2, w_in, w_pool,
           pool_scale, lambda_q1, lambda_k1, lambda_q2, lambda_k2, subln_g, w_out, ffn2_w1,
           ffn2_w3, ffn2_w2):
    b, s, d = x.shape
    depth = w_ada.shape[0]
    d_pool = pool_scale.shape[1]
    head_dim = lambda_q1.shape[1]
    d_in = w_in.shape[2]
    d_qk = (d_in - d_pool) // 3
    n_heads = d_qk // (2 * head_dim)
    dff = ffn1_w1.shape[2]
    ffn_tk = FFN_COLS if dff % FFN_COLS == 0 else dff
    xr = x.reshape(b * s, d)
    h = None
    for l in range(depth):
        lambda_init = 0.8 - 0.6 * math.exp(-0.3 * l)
        c_pad = jnp.pad(c, ((0, -b % F32_SUBLANES), (0, 0)))
        b_ada_row = b_ada[l].reshape(1, -1)
        n_early = N_EARLY_MOD * d
        mod_early = _ada(c_pad, w_ada[l], b_ada_row, n_early)
        early = lambda i: mod_early[:b, i * d:(i + 1) * d].reshape(b, 1, d)
        pre_g = pre_norm_g[l].reshape(N_SUB, 1, d)
        post_g = post_norm_g[l].reshape(N_SUB, 1, d)

        if h is None:
            h = _prenorm(xr, early(0), early(1), pre_g[0], s)
        act = _ffn_up(h, ffn1_w1[l], ffn1_w3[l], s)
        xr, h = _down(act, ffn1_w2[l], xr, early(2), post_g[0], (early(3), early(4), pre_g[1]),
                      coef=0.5, tk=ffn_tk, seq=s)

        u, qkv = _proj(h, w_in[l], d_pool, d_qk, head_dim, s)
        mixed, mod_late = _mix(u, qkv, w_pool[l], pool_scale[l], lambda_q1[l], lambda_k1[l],
                               lambda_q2[l], lambda_k2[l], subln_g[l], c_pad, w_ada[l], b_ada_row,
                               n_early, n_heads, head_dim, s, lambda_init)
        late = lambda i: mod_late[:b, (i - N_EARLY_MOD) * d:(i + 1 - N_EARLY_MOD) * d].reshape(b, 1, d)
        xr, h = _down(mixed, w_out[l], xr, late(5), post_g[1], (late(6), late(7), pre_g[2]),
                      coef=1.0, tk=min(FFN_COLS, d), seq=s)

        act = _ffn_up(h, ffn2_w1[l], ffn2_w3[l], s)
        xr, h = _down(act, ffn2_w2[l], xr, late(8), post_g[2], None,
                      coef=0.5, tk=ffn_tk, seq=s)
    return xr.reshape(b, s, d)
```

```python
import functools
import math

import jax
import jax.numpy as jnp
from jax import lax
from jax.experimental import pallas as pl
from jax.experimental.pallas import tpu as pltpu

F32 = jnp.float32
BF16 = jnp.bfloat16

NORM_EPS = 1e-6
POOL_WINDOWS = (2, 4, 8, 16)
N_SUB = 3
N_EARLY_MOD = 5
V7X_VMEM_LIMIT_BYTES = 58 * 1024 * 1024
F32_SUBLANES = 8
NORM_ROWS = 8
ROW_TILE = 2048
PRENORM_ROWS = 512
EPILOGUE_ROWS = 128
UP_ROW_CHUNK = 512
X_SLOTS = 3
ADA_COLS = 512
FFN_COLS = 256
PROJ_COLS = 512
ATTN_Q_ROWS = 128
ATTN_SLAB_ROWS = 32


def _compiler_params(n_grid_dims):
    return pltpu.CompilerParams(
        dimension_semantics=("arbitrary",) * n_grid_dims,
        vmem_limit_bytes=V7X_VMEM_LIMIT_BYTES,
    )


def _inv_rms(xf):
    return lax.rsqrt(jnp.mean(xf * xf, axis=-1, keepdims=True) + NORM_EPS)


def _bdot(a, b):
    return jnp.dot(a, b, preferred_element_type=F32)


def _ada_block(c_ref, w_ref, b_ref, o_ref):
    c = c_ref[...]
    act = (c * jax.nn.sigmoid(c)).astype(BF16)
    o_ref[...] = _bdot(act, w_ref[...].astype(BF16)) + b_ref[...]


def _ada(c_pad, w_ada, b_ada, n_cols):
    rows, d = c_pad.shape
    tn = ADA_COLS if n_cols % ADA_COLS == 0 else n_cols
    return pl.pallas_call(
        _ada_block,
        grid=(n_cols // tn,),
        in_specs=[
            pl.BlockSpec((rows, d), lambda j: (0, 0)),
            pl.BlockSpec((d, tn), lambda j: (0, j)),
            pl.BlockSpec((1, tn), lambda j: (0, j)),
        ],
        out_specs=pl.BlockSpec((rows, tn), lambda j: (0, j)),
        out_shape=jax.ShapeDtypeStruct((rows, n_cols), F32),
        compiler_params=_compiler_params(1),
        name="ada",
    )(c_pad, w_ada, b_ada)


def _modulated_norm(xf, gs, shift):
    return ((xf * _inv_rms(xf)) * gs + shift).astype(BF16)


def _prenorm_kernel(x_ref, shift_ref, scale_ref, g_ref, h_ref):
    gs = g_ref[...] * (1.0 + scale_ref[0])
    shift = shift_ref[0]
    for r in range(0, x_ref.shape[0], NORM_ROWS):
        rows = slice(r, r + NORM_ROWS)
        h_ref[rows, :] = _modulated_norm(x_ref[rows, :], gs, shift)


def _prenorm(x, shift, scale, g, seq):
    m, d = x.shape
    tm = min(PRENORM_ROWS, seq)
    return pl.pallas_call(
        _prenorm_kernel,
        grid=(m // tm,),
        in_specs=[
            pl.BlockSpec((tm, d), lambda i: (i, 0)),
            pl.BlockSpec((1, 1, d), lambda i: (i * tm // seq, 0, 0)),
            pl.BlockSpec((1, 1, d), lambda i: (i * tm // seq, 0, 0)),
            pl.BlockSpec((1, d), lambda i: (0, 0)),
        ],
        out_specs=pl.BlockSpec((tm, d), lambda i: (i, 0)),
        out_shape=jax.ShapeDtypeStruct((m, d), BF16),
        compiler_params=_compiler_params(1),
        name="prenorm",
    )(x, shift, scale, g)


def _ffn_up_kernel(h_ref, w1_ref, w3_ref, act_ref):
    w1 = w1_ref[...].astype(BF16)
    w3 = w3_ref[...].astype(BF16)
    chunk = min(UP_ROW_CHUNK, h_ref.shape[0])
    for r in range(0, h_ref.shape[0], chunk):
        h = h_ref[r:r + chunk, :]
        a = _bdot(h, w1)
        b = _bdot(h, w3)
        act_ref[r:r + chunk, :] = (a * jax.nn.sigmoid(a) * b).astype(BF16)


def _ffn_up(h, w1, w3, seq):
    m, d = h.shape
    dff = w1.shape[1]
    tm = min(ROW_TILE, seq)
    tn = FFN_COLS if dff % FFN_COLS == 0 else dff
    return pl.pallas_call(
        _ffn_up_kernel,
        grid=(m // tm, dff // tn),
        in_specs=[
            pl.BlockSpec((tm, d), lambda i, j: (i, 0)),
            pl.BlockSpec((d, tn), lambda i, j: (0, j)),
            pl.BlockSpec((d, tn), lambda i, j: (0, j)),
        ],
        out_specs=pl.BlockSpec((tm, tn), lambda i, j: (i, j)),
        out_shape=jax.ShapeDtypeStruct((m, dff), BF16),
        compiler_params=_compiler_params(2),
        name="ffn_up",
    )(h, w1, w3)


def _down_kernel(act_ref, w_ref, gate_ref, post_g_ref, *rest, n_k, coef, emit_h):
    if emit_h:
        (shift_ref, scale_ref, g_ref, x_hbm, o_hbm, h_hbm,
         acc_ref, xbuf, obuf, hbuf, sem_x, sem_o, sem_h) = rest
    else:
        x_hbm, o_hbm, acc_ref, xbuf, obuf, sem_x, sem_o = rest
    k = pl.program_id(1)
    tm = acc_ref.shape[0]
    te = xbuf.shape[1]
    n_chunks = tm // te
    row0 = pl.program_id(0) * tm

    def chunk_rows(c):
        return pl.ds(pl.multiple_of(row0 + c * te, te), te)

    def x_copy(c, slot):
        return pltpu.make_async_copy(x_hbm.at[chunk_rows(c), :], xbuf.at[slot], sem_x.at[slot])

    def o_copy(c, slot):
        return pltpu.make_async_copy(obuf.at[slot], o_hbm.at[chunk_rows(c), :], sem_o.at[slot])

    def h_copy(c, slot):
        return pltpu.make_async_copy(hbuf.at[slot], h_hbm.at[chunk_rows(c), :], sem_h.at[slot])

    def wait_results(c, slot):
        o_copy(c, slot).wait()
        if emit_h:
            h_copy(c, slot).wait()

    @pl.when(k == n_k - 1)
    def _():
        for c in range(min(X_SLOTS, n_chunks)):
            x_copy(c, c).start()

    @pl.when(k == 0)
    def _():
        acc_ref[...] = _bdot(act_ref[...], w_ref[...].astype(BF16))

    @pl.when((k > 0) & (k < n_k))
    def _():
        acc_ref[...] += _bdot(act_ref[...], w_ref[...].astype(BF16))

    @pl.when(k == n_k)
    def _():
        cg = coef * gate_ref[0]
        pg = post_g_ref[...]
        if emit_h:
            gs = g_ref[...] * (1.0 + scale_ref[0])
            shift = shift_ref[0]

        def chunk(c, carry):
            xs = c % X_SLOTS
            rs = c % 2
            x_copy(c, xs).wait()

            @pl.when(c >= 2)
            def _():
                wait_results(c - 2, rs)

            base = c * te
            for r in range(0, te, NORM_ROWS):
                rows = slice(r, r + NORM_ROWS)
                f = acc_ref[pl.ds(pl.multiple_of(base + r, NORM_ROWS), NORM_ROWS), :]
                out = xbuf[xs, rows, :] + cg * ((f * _inv_rms(f)) * pg)
                obuf[rs, rows, :] = out
                if emit_h:
                    hbuf[rs, rows, :] = _modulated_norm(out, gs, shift)
            o_copy(c, rs).start()
            if emit_h:
                h_copy(c, rs).start()

            @pl.when(c + X_SLOTS < n_chunks)
            def _():
                x_copy(c + X_SLOTS, xs).start()

            return carry

        lax.fori_loop(0, n_chunks, chunk, 0)
        for c in range(max(n_chunks - 2, 0), n_chunks):
            wait_results(c, c % 2)


def _down(act, w, x, gate, post_g, next_mod, *, coef, tk, seq):
    m, d = x.shape
    kdim = act.shape[1]
    tm = min(ROW_TILE, seq)
    te = min(EPILOGUE_ROWS, tm)
    assert kdim % tk == 0 and tm % te == 0
    n_k = kdim // tk
    emit_h = next_mod is not None
    k_tile = lambda k: jnp.minimum(k, n_k - 1)
    batch_vec = pl.BlockSpec((1, 1, d), lambda i, k: (i * tm // seq, 0, 0))
    row_vec = pl.BlockSpec((1, d), lambda i, k: (0, 0))
    hbm = pl.BlockSpec(memory_space=pl.ANY)
    in_specs = [
        pl.BlockSpec((tm, tk), lambda i, k: (i, k_tile(k))),
        pl.BlockSpec((tk, d), lambda i, k: (k_tile(k), 0)),
        batch_vec, row_vec,
    ]
    args = [act, w, gate, post_g]
    if emit_h:
        in_specs += [batch_vec, batch_vec, row_vec]
        args += list(next_mod)
    in_specs.append(hbm)
    args.append(x)
    out_specs = [hbm]
    out_shape = [jax.ShapeDtypeStruct((m, d), F32)]
    scratch = [pltpu.VMEM((tm, d), F32), pltpu.VMEM((X_SLOTS, te, d), F32),
               pltpu.VMEM((2, te, d), F32)]
    sems = [pltpu.SemaphoreType.DMA((X_SLOTS,)), pltpu.SemaphoreType.DMA((2,))]
    if emit_h:
        out_specs.append(hbm)
        out_shape.append(jax.ShapeDtypeStruct((m, d), BF16))
        scratch.append(pltpu.VMEM((2, te, d), BF16))
        sems.append(pltpu.SemaphoreType.DMA((2,)))
    kern = functools.partial(_down_kernel, n_k=n_k, coef=coef, emit_h=emit_h)
    res = pl.pallas_call(
        kern,
        grid=(m // tm, n_k + 1),
        in_specs=in_specs,
        out_specs=out_specs,
        out_shape=out_shape,
        scratch_shapes=scratch + sems,
        compiler_params=_compiler_params(2),
        name="down",
    )(*args)
    return (res[0], res[1]) if emit_h else (res[0], None)


def _proj_kernel(h_ref, w_ref, u_ref, qkv_ref, *, n_u_tiles, n_q_tiles, q_scale):
    j = pl.program_id(1)

    @pl.when(j < n_u_tiles)
    def _():
        u_ref[...] = _bdot(h_ref[...], w_ref[...].astype(BF16))

    @pl.when(j >= n_u_tiles)
    def _():
        factor = jnp.where(j < n_u_tiles + n_q_tiles, q_scale, 1.0).astype(F32)
        qkv_ref[...] = (_bdot(h_ref[...], w_ref[...].astype(BF16)) * factor).astype(BF16)


def _proj(h, w_in, d_pool, d_qk, head_dim, seq):
    m, d = h.shape
    d_in = w_in.shape[1]
    tn = min(PROJ_COLS, d_pool)
    assert d_pool % tn == 0 and d_qk % tn == 0
    n_u = d_pool // tn
    tm = min(ROW_TILE, seq)
    kern = functools.partial(_proj_kernel, n_u_tiles=n_u, n_q_tiles=d_qk // tn,
                             q_scale=math.log2(math.e) * head_dim ** -0.5)
    return pl.pallas_call(
        kern,
        grid=(m // tm, d_in // tn),
        in_specs=[
            pl.BlockSpec((tm, d), lambda i, j: (i, 0), pipeline_mode=pl.Buffered(1)),
            pl.BlockSpec((d, tn), lambda i, j: (0, j)),
        ],
        out_specs=[
            pl.BlockSpec((tm, tn), lambda i, j: (i, jnp.minimum(j, n_u - 1))),
            pl.BlockSpec((tm, tn), lambda i, j: (i, jnp.maximum(j - n_u, 0))),
        ],
        out_shape=[
            jax.ShapeDtypeStruct((m, d_pool), F32),
            jax.ShapeDtypeStruct((m, d_in - d_pool), BF16),
        ],
        compiler_params=_compiler_params(2),
        name="proj",
    )(h, w_in)


def _pool_group(u_ref, w_ref, ps_ref, o_ref, group):
    win = POOL_WINDOWS[group]
    ug = u_ref[...]
    t = lax.broadcasted_iota(jnp.int32, ug.shape, 0)
    s = ug
    k = 1
    while k < win:
        s = s + jnp.where(t >= k, pltpu.roll(s, k, axis=0), 0.0)
        k *= 2
    count = jnp.minimum(t + 1, win).astype(F32)
    y = (s / count - ug).astype(BF16)
    o_ref[...] = (_bdot(y, w_ref[group].astype(BF16)) * ps_ref[...]).astype(BF16)


def _diff_attention(lam, sg_ref, q_ref, k_ref, v_ref, o_ref, s_scr, a_scr, lambda_init, tq):
    seq = q_ref.shape[0]
    hd = q_ref.shape[1] // 2
    sg = sg_ref[...]
    nt_dims = (((1,), (1,)), ((), ()))
    slab_col = lax.broadcasted_iota(jnp.int32, (ATTN_SLAB_ROWS, tq), 1)
    slab_row = lax.broadcasted_iota(jnp.int32, (ATTN_SLAB_ROWS, tq), 0)
    n_blocks = seq // tq

    def scores(i):
        kv = (i + 1) * tq
        for m, cols in enumerate((slice(0, hd), slice(hd, 2 * hd))):
            s_scr[i % 2, m, :, 0:kv] = lax.dot_general(
                q_ref[i * tq:kv, cols], k_ref[0:kv, cols], nt_dims, preferred_element_type=F32)
        return i

    def exp_rows(i, m, rows):
        kv = (i + 1) * tq
        s = s_scr[i % 2, m, rows, 0:kv]
        diag = jnp.where(slab_col <= slab_row + rows.start, s[:, kv - tq:], -jnp.inf)
        s = diag if i == 0 else jnp.concatenate([s[:, :kv - tq], diag], axis=1)
        p = jnp.exp2(s - jnp.max(s, axis=-1, keepdims=True))
        return p, jnp.sum(p, axis=-1, keepdims=True)

    def weights(i, _):
        kv = (i + 1) * tq
        inv = []
        for r in range(0, tq, ATTN_SLAB_ROWS):
            rows = slice(r, r + ATTN_SLAB_ROWS)
            p1, l1 = exp_rows(i, 0, rows)
            p2, l2 = exp_rows(i, 1, rows)
            a_scr[i % 2, rows, 0:kv] = (p1 - p2 * (lam * l1 / l2)).astype(BF16)
            inv.append(1.0 / l1)
        return (jnp.concatenate(inv, axis=0),)

    def finish(i, inv_l1):
        kv = (i + 1) * tq
        o = _bdot(a_scr[i % 2, :, 0:kv], v_ref[0:kv, :]) * inv_l1
        o_ref[i * tq:kv, :] = (((o * _inv_rms(o)) * sg) * (1.0 - lambda_init)).astype(BF16)

    s_next = scores(0)
    pending = None
    for i in range(n_blocks):
        s_cur = s_next
        if i + 1 < n_blocks:
            s_next = scores(i + 1)
        cur = (i,) + weights(i, s_cur)
        if pending is not None:
            finish(*pending)
        pending = cur
    finish(*pending)


def _mix_kernel(lq1_ref, lk1_ref, lq2_ref, lk2_ref, sg_ref, u_ref, wp_ref, ps_ref,
                q_ref, k_ref, v_ref, c_ref, wa_ref, ba_ref, o_ref, mod_ref, s_scr, a_scr,
                *, lambda_init, tq):
    n_groups = len(POOL_WINDOWS)
    hh = pl.program_id(1)

    for g in range(n_groups):
        @pl.when(hh == g)
        def _(g=g):
            _ada_block(c_ref, wa_ref, ba_ref, mod_ref)
            _pool_group(u_ref, wp_ref, ps_ref, o_ref, g)

    @pl.when(hh >= n_groups)
    def _():
        _ada_block(c_ref, wa_ref, ba_ref, mod_ref)
        lam = (jnp.exp(jnp.sum(lq1_ref[...] * lk1_ref[...], axis=-1, keepdims=True))
               - jnp.exp(jnp.sum(lq2_ref[...] * lk2_ref[...], axis=-1, keepdims=True))
               + lambda_init)
        _diff_attention(lam, sg_ref, q_ref, k_ref, v_ref, o_ref, s_scr, a_scr, lambda_init, tq)


def _mix(u, qkv, w_pool, pool_scale, lq1, lk1, lq2, lk2, subln_g, c_pad, w_ada, b_ada, ada_col0,
         n_heads, head_dim, seq, lambda_init):
    m, d_pool = u.shape
    n_groups = len(POOL_WINDOWS)
    hw = 2 * head_dim
    assert d_pool == n_groups * hw and w_pool.shape == (n_groups, hw, hw)
    n_b, n_hh = m // seq, n_groups + n_heads
    rows, d = c_pad.shape
    n_late = w_ada.shape[1] - ada_col0
    ta = n_late // (n_b * n_hh)
    assert ta * n_b * n_hh == n_late and ta % 128 == 0 and ada_col0 % ta == 0
    vec = lambda n: pl.BlockSpec((1, n), lambda b, hh: (0, 0))
    head = lambda hh: jnp.maximum(hh - n_groups, 0)
    group = lambda hh: jnp.minimum(hh, n_groups - 1)
    ada_block = lambda b, hh: (0, ada_col0 // ta + b * n_hh + hh)
    tq = min(ATTN_Q_ROWS, seq)
    kern = functools.partial(_mix_kernel, lambda_init=lambda_init, tq=tq)
    return pl.pallas_call(
        kern,
        grid=(n_b, n_hh),
        in_specs=[
            vec(head_dim), vec(head_dim), vec(head_dim), vec(head_dim), vec(hw),
            pl.BlockSpec((seq, hw), lambda b, hh: (b, group(hh))),
            pl.BlockSpec(w_pool.shape, lambda b, hh: (0, 0, 0)),
            pl.BlockSpec((1, hw), lambda b, hh: (0, group(hh))),
            pl.BlockSpec((seq, hw), lambda b, hh: (b, head(hh))),
            pl.BlockSpec((seq, hw), lambda b, hh: (b, n_heads + head(hh))),
            pl.BlockSpec((seq, hw), lambda b, hh: (b, 2 * n_heads + head(hh))),
            pl.BlockSpec((rows, d), lambda b, hh: (0, 0)),
            pl.BlockSpec((d, ta), ada_block),
            pl.BlockSpec((1, ta), ada_block),
        ],
        out_specs=[
            pl.BlockSpec((seq, hw), lambda b, hh: (b, hh)),
            pl.BlockSpec((rows, ta), lambda b, hh: (0, b * n_hh + hh)),
        ],
        out_shape=[
            jax.ShapeDtypeStruct((m, d_pool + n_heads * hw), BF16),
            jax.ShapeDtypeStruct((rows, n_late), F32),
        ],
        scratch_shapes=[pltpu.VMEM((2, 2, tq, seq), F32),
                        pltpu.VMEM((2, tq, seq), BF16)],
        compiler_params=_compiler_params(2),
        name="mix",
    )(lq1.reshape(1, -1), lk1.reshape(1, -1), lq2.reshape(1, -1), lk2.reshape(1, -1),
      subln_g.reshape(1, -1), u, w_pool, pool_scale.reshape(1, d_pool), qkv, qkv, qkv,
      c_pad, w_ada, b_ada)


def kernel(x, c, w_ada, b_ada, pre_norm_g, post_norm_g, ffn1_w1, ffn1_w3, ffn1_w2, w_in, w_pool,
           pool_scale, lambda_q1, lambda_k1, lambda_q2, lambda_k2, subln_g, w_out, ffn2_w1,
           ffn2_w3, ffn2_w2):
    b, s, d = x.shape
    depth = w_ada.shape[0]
    d_pool = pool_scale.shape[1]
    head_dim = lambda_q1.shape[1]
    d_in = w_in.shape[2]
    d_qk = (d_in - d_pool) // 3
    n_heads = d_qk // (2 * head_dim)
    dff = ffn1_w1.shape[2]
    ffn_tk = FFN_COLS if dff % FFN_COLS == 0 else dff
    xr = x.reshape(b * s, d)
    h = None
    for l in range(depth):
        lambda_init = 0.8 - 0.6 * math.exp(-0.3 * l)
        c_pad = jnp.pad(c, ((0, -b % F32_SUBLANES), (0, 0)))
        b_ada_row = b_ada[l].reshape(1, -1)
        n_early = N_EARLY_MOD * d
        mod_early = _ada(c_pad, w_ada[l], b_ada_row, n_early)
        early = lambda i: mod_early[:b, i * d:(i + 1) * d].reshape(b, 1, d)
        pre_g = pre_norm_g[l].reshape(N_SUB, 1, d)
        post_g = post_norm_g[l].reshape(N_SUB, 1, d)

        if h is None:
            h = _prenorm(xr, early(0), early(1), pre_g[0], s)
        act = _ffn_up(h, ffn1_w1[l], ffn1_w3[l], s)
        xr, h = _down(act, ffn1_w2[l], xr, early(2), post_g[0], (early(3), early(4), pre_g[1]),
                      coef=0.5, tk=ffn_tk, seq=s)

        u, qkv = _proj(h, w_in[l], d_pool, d_qk, head_dim, s)
        mixed, mod_late = _mix(u, qkv, w_pool[l], pool_scale[l], lambda_q1[l], lambda_k1[l],
                               lambda_q2[l], lambda_k2[l], subln_g[l], c_pad, w_ada[l], b_ada_row,
                               n_early, n_heads, head_dim, s, lambda_init)
        late = lambda i: mod_late[:b, (i - N_EARLY_MOD) * d:(i + 1 - N_EARLY_MOD) * d].reshape(b, 1, d)
        xr, h = _down(mixed, w_out[l], xr, late(5), post_g[1], (late(6), late(7), pre_g[2]),
                      coef=1.0, tk=min(FFN_COLS, d), seq=s)

        act = _ffn_up(h, ffn2_w1[l], ffn2_w3[l], s)
        xr, h = _down(act, ffn2_w2[l], xr, late(8), post_g[2], None,
                      coef=0.5, tk=ffn_tk, seq=s)
    return xr.reshape(b, s, d)
```
